```python
import jax, jax.numpy as jnp
from jax import lax
import numpy as np

D_MODEL = 1024
BATCH = 8
SEQ = 4096
DEPTH = 2

N_MIXERS = 2
FNET_GROUPS = 8
FNET_GROUP_DIM = D_MODEL // FNET_GROUPS
HEAD_DIM = 64
N_HEADS = D_MODEL // HEAD_DIM
N_KV_HEADS = 4
Q_PER_KV = N_HEADS // N_KV_HEADS
WINDOW = 128
ATTN_BLOCK = 128
ROPE_THETA = 10000.0
N_EXPERTS = 32
TOP_K = 4
D_FF = D_MODEL
SWIGLU_ALPHA = 1.702
SWIGLU_LIMIT = 7.0
MOE_BLOCK = 128
RMS_EPS = 1e-5
NEG_INF = -1e30
N_FNET_LAYERS = (DEPTH + 1) // 2
N_ATTN_LAYERS = DEPTH // 2

kernel_name = 'hybrid_fnet_swa_moe_encoder'


def rms_norm(x, g):
    xf = x.astype(jnp.float32)
    y = xf * lax.rsqrt(jnp.mean(xf * xf, axis=-1, keepdims=True) + RMS_EPS)
    return (y * g.astype(jnp.float32)).astype(x.dtype)


def fourier_mixer(h, w_o, b_o):
    B, S, D = h.shape
    hg = h.astype(jnp.float32).reshape(B, S, FNET_GROUPS, FNET_GROUP_DIM)
    mixed = jnp.fft.fft2(hg, axes=(1, 3), norm='ortho').real
    mixed = mixed.reshape(B, S, D).astype(h.dtype)
    return mixed @ w_o + b_o


def apply_rope(t, cos, sin):
    t1, t2 = jnp.split(t, 2, axis=-1)
    c = cos[None, :, None, :]
    s = sin[None, :, None, :]
    return jnp.concatenate([t1 * c - t2 * s, t2 * c + t1 * s], axis=-1)


def window_attention(h, w_qkv, b_qkv, sink, w_o, b_o):
    B, S, _ = h.shape
    nq = S // ATTN_BLOCK
    qkv = h @ w_qkv + b_qkv
    q, k, v = jnp.split(qkv, [N_HEADS * HEAD_DIM, (N_HEADS + N_KV_HEADS) * HEAD_DIM], axis=-1)
    q = q.reshape(B, S, N_HEADS, HEAD_DIM)
    k = k.reshape(B, S, N_KV_HEADS, HEAD_DIM)
    v = v.reshape(B, S, N_KV_HEADS, HEAD_DIM)
    pos = jnp.arange(S, dtype=jnp.float32)
    inv_freq = ROPE_THETA ** (-jnp.arange(0, HEAD_DIM, 2, dtype=jnp.float32) / HEAD_DIM)
    ang = pos[:, None] * inv_freq[None, :]
    cos = jnp.cos(ang).astype(h.dtype)
    sin = jnp.sin(ang).astype(h.dtype)
    q = apply_rope(q, cos, sin)
    k = apply_rope(k, cos, sin)
    qb = jnp.moveaxis(q.reshape(B, nq, ATTN_BLOCK, N_KV_HEADS, Q_PER_KV, HEAD_DIM), 1, 0)

    def band(t):
        tb = t.reshape(B, nq, ATTN_BLOCK, N_KV_HEADS, HEAD_DIM)
        tp = jnp.pad(tb, ((0, 0), (1, 1), (0, 0), (0, 0), (0, 0)))
        cat = jnp.concatenate([tp[:, :-2], tp[:, 1:-1], tp[:, 2:]], axis=2)
        return jnp.moveaxis(cat, 1, 0)

    kb = band(k)
    vb = band(v)
    blk = jnp.arange(nq)[:, None, None]
    qpos = blk * ATTN_BLOCK + jnp.arange(ATTN_BLOCK)[None, :, None]
    kpos = (blk - 1) * ATTN_BLOCK + jnp.arange(3 * ATTN_BLOCK)[None, None, :]
    mask = (jnp.abs(qpos - kpos) <= WINDOW) & (kpos >= 0) & (kpos < S)
    sink_l = sink.astype(jnp.float32).reshape(1, N_KV_HEADS, Q_PER_KV, 1, 1)
    scale = HEAD_DIM ** -0.5

    def attend(args):
        q_blk, k_blk, v_blk, m_blk = args
        s = jnp.einsum('bqkgd,bpkd->bkgqp', q_blk, k_blk).astype(jnp.float32) * scale
        s = jnp.where(m_blk[None, None, None], s, NEG_INF)
        mx = jnp.maximum(jnp.max(s, axis=-1, keepdims=True), sink_l)
        p = jnp.exp(s - mx)
        denom = jnp.sum(p, axis=-1, keepdims=True) + jnp.exp(sink_l - mx)
        probs = (p / denom).astype(v_blk.dtype)
        return jnp.einsum('bkgqp,bpkd->bqkgd', probs, v_blk)

    o = lax.map(attend, (qb, kb, vb, mask))
    o = jnp.moveaxis(o, 0, 1).reshape(B, S, N_HEADS * HEAD_DIM)
    return o @ w_o + b_o


def moe_ffn(h, router_w, router_b, w_gu, b_gu, w_down, b_down):
    B, S, D = h.shape
    T = B * S
    xt = h.reshape(T, D)
    logits = xt.astype(jnp.float32) @ router_w.astype(jnp.float32) + router_b.astype(jnp.float32)
    top_vals, top_idx = lax.top_k(logits, TOP_K)
    gates = jax.nn.softmax(top_vals, axis=-1)
    A = T * TOP_K
    flat_e = top_idx.reshape(A)
    flat_tok = jnp.arange(A, dtype=jnp.int32) // TOP_K
    order = jnp.argsort(flat_e, stable=True)
    sorted_e = flat_e[order]
    sorted_tok = flat_tok[order]
    sorted_gate = gates.reshape(A)[order]
    counts = jnp.bincount(flat_e, length=N_EXPERTS)
    padded = ((counts + MOE_BLOCK - 1) // MOE_BLOCK) * MOE_BLOCK
    starts = jnp.cumsum(counts) - counts
    pend = jnp.cumsum(padded)
    pstarts = pend - padded
    rank = jnp.arange(A, dtype=jnp.int32) - starts[sorted_e]
    dest = pstarts[sorted_e] + rank
    n_slots = A + N_EXPERTS * MOE_BLOCK
    n_blocks = n_slots // MOE_BLOCK
    slot_tok = jnp.full((n_slots,), T, dtype=jnp.int32).at[dest].set(sorted_tok)
    x_pad = jnp.concatenate([xt, jnp.zeros((1, D), xt.dtype)], axis=0)
    xs = x_pad[slot_tok].reshape(n_blocks, MOE_BLOCK, D)
    block_start = jnp.arange(n_blocks, dtype=jnp.int32) * MOE_BLOCK
    block_expert = jnp.clip(jnp.searchsorted(pend, block_start, side='right'), 0, N_EXPERTS - 1)

    def expert_block(args):
        xb, e = args
        gu = xb @ w_gu[e] + b_gu[e]
        gate, up = jnp.split(gu, 2, axis=-1)
        gate = jnp.minimum(gate, SWIGLU_LIMIT)
        up = jnp.clip(up, -SWIGLU_LIMIT, SWIGLU_LIMIT)
        glu = gate * jax.nn.sigmoid(gate * SWIGLU_ALPHA)
        return ((up + 1) * glu) @ w_down[e] + b_down[e]

    ys = lax.map(expert_block, (xs, block_expert)).reshape(n_slots, D)
    y_assign = ys[dest] * sorted_gate[:, None].astype(ys.dtype)
    out = jax.ops.segment_sum(y_assign, sorted_tok, num_segments=T)
    return out.reshape(B, S, D).astype(h.dtype)


def setup_inputs(seed: int = 0) -> dict:
    key = jax.random.key(seed)
    ks = jax.random.split(key, 18)
    f32 = jnp.float32
    qkv_dim = (N_HEADS + 2 * N_KV_HEADS) * HEAD_DIM
    nrm = lambda k, shape, s: jax.random.normal(k, shape, f32) * s
    return {
        'x': nrm(ks[0], (BATCH, SEQ, D_MODEL), 1.0),
        'norm_mix_g': 1.0 + nrm(ks[1], (DEPTH, D_MODEL), 0.02),
        'norm_ffn_g': 1.0 + nrm(ks[2], (DEPTH, D_MODEL), 0.02),
        'fnet_w_o': nrm(ks[3], (N_FNET_LAYERS, D_MODEL, D_MODEL), D_MODEL ** -0.5),
        'fnet_b_o': nrm(ks[4], (N_FNET_LAYERS, D_MODEL), 0.02),
        'attn_w_qkv': nrm(ks[5], (N_ATTN_LAYERS, D_MODEL, qkv_dim), D_MODEL ** -0.5),
        'attn_b_qkv': nrm(ks[6], (N_ATTN_LAYERS, qkv_dim), 0.02),
        'attn_sink': nrm(ks[7], (N_ATTN_LAYERS, N_HEADS), 0.5),
        'attn_w_o': nrm(ks[8], (N_ATTN_LAYERS, N_HEADS * HEAD_DIM, D_MODEL), (N_HEADS * HEAD_DIM) ** -0.5),
        'attn_b_o': nrm(ks[9], (N_ATTN_LAYERS, D_MODEL), 0.02),
        'router_w': nrm(ks[10], (DEPTH, D_MODEL, N_EXPERTS), D_MODEL ** -0.5),
        'router_b': nrm(ks[11], (DEPTH, N_EXPERTS), 0.01),
        'expert_w_gu': nrm(ks[12], (DEPTH, N_EXPERTS, D_MODEL, 2 * D_FF), D_MODEL ** -0.5),
        'expert_b_gu': nrm(ks[13], (DEPTH, N_EXPERTS, 2 * D_FF), 0.02),
        'expert_w_down': nrm(ks[14], (DEPTH, N_EXPERTS, D_FF, D_MODEL), D_FF ** -0.5),
        'expert_b_down': nrm(ks[15], (DEPTH, N_EXPERTS, D_MODEL), 0.02),
        'final_norm_g': 1.0 + nrm(ks[16], (D_MODEL,), 0.02),
    }


def reference(x, norm_mix_g, norm_ffn_g, fnet_w_o, fnet_b_o, attn_w_qkv, attn_b_qkv, attn_sink, attn_w_o, attn_b_o, router_w, router_b, expert_w_gu, expert_b_gu, expert_w_down, expert_b_down, final_norm_g):
    for i in range(DEPTH):
        h = rms_norm(x, norm_mix_g[i])
        j = i // N_MIXERS
        if i % N_MIXERS == 0:
            mix = fourier_mixer(h, fnet_w_o[j], fnet_b_o[j])
        else:
            mix = window_attention(h, attn_w_qkv[j], attn_b_qkv[j], attn_sink[j], attn_w_o[j], attn_b_o[j])
        x = x + mix
        h = rms_norm(x, norm_ffn_g[i])
        x = x + moe_ffn(h, router_w[i], router_b[i], expert_w_gu[i], expert_b_gu[i], expert_w_down[i], expert_b_down[i])
    return rms_norm(x, final_norm_g)
```

```python
import functools
import math

import jax
import jax.numpy as jnp
from jax import lax
from jax.experimental import pallas as pl
from jax.experimental.pallas import tpu as pltpu

F32 = jnp.float32
BF16 = jnp.bfloat16
I32 = jnp.int32

D_MODEL = 1024
FNET_GROUPS = 8
FNET_GROUP_DIM = D_MODEL // FNET_GROUPS
HEAD_DIM = 64
N_HEADS = 16
N_KV_HEADS = 4
Q_PER_KV = N_HEADS // N_KV_HEADS
KV_DIM = N_KV_HEADS * HEAD_DIM
WINDOW = 128
ATTN_BLOCK = 128
ROPE_THETA = 10000.0
N_EXPERTS = 32
TOP_K = 4
D_FF = D_MODEL
SWIGLU_ALPHA = 1.702
SWIGLU_LIMIT = 7.0
RMS_EPS = 1e-5
NEG_INF = -1e30

LANES = 128
VMEM_LIMIT = 56 * 1024 * 1024

ROW_TILE = 512
MOE_BM = 256
COMBINE_TILE = 128
DFT_TM = 1024
DFT_TK = 512

_ARB = pltpu.ARBITRARY


def _params(n_axes):
    return pltpu.CompilerParams(dimension_semantics=(_ARB,) * n_axes, vmem_limit_bytes=VMEM_LIMIT)


def _rms(x, g):
    ms = jnp.mean(x * x, axis=-1, keepdims=True)
    return (x * lax.rsqrt(ms + RMS_EPS)) * g


def _fnet_chan_kernel(x_ref, g_ref, cs_ref, p_ref, q_ref):
    h = _rms(x_ref[...], g_ref[...]).astype(BF16)
    for gi in range(FNET_GROUPS):
        lo = gi * FNET_GROUP_DIM
        pq = jnp.dot(h[:, lo:lo + FNET_GROUP_DIM], cs_ref[...], preferred_element_type=F32)
        p_ref[:, lo:lo + FNET_GROUP_DIM] = pq[:, :FNET_GROUP_DIM].astype(BF16)
        q_ref[:, lo:lo + FNET_GROUP_DIM] = pq[:, FNET_GROUP_DIM:].astype(BF16)


def _fnet_chan(x2d, g, cs):
    t, d = x2d.shape
    return pl.pallas_call(
        _fnet_chan_kernel,
        grid=(t // ROW_TILE,),
        in_specs=[
            pl.BlockSpec((ROW_TILE, d), lambda i: (i, 0)),
            pl.BlockSpec((1, d), lambda i: (0, 0)),
            pl.BlockSpec(cs.shape, lambda i: (0, 0)),
        ],
        out_specs=[pl.BlockSpec((ROW_TILE, d), lambda i: (i, 0))] * 2,
        out_shape=[jax.ShapeDtypeStruct((t, d), BF16)] * 2,
        compiler_params=_params(1),
        name="fnet_chan",
    )(x2d, g, cs)


def _fnet_seq_kernel(ac_ref, as_ref, p_ref, q_ref, wo_ref, bo_ref, x_ref, o_ref, acc_ref):
    k = pl.program_id(2)

    @pl.when(k == 0)
    def _():
        acc_ref[...] = jnp.zeros_like(acc_ref)

    acc_ref[...] += (jnp.dot(ac_ref[...], p_ref[...], preferred_element_type=F32)
                     + jnp.dot(as_ref[...], q_ref[...], preferred_element_type=F32))

    @pl.when(k == pl.num_programs(2) - 1)
    def _():
        mixed = acc_ref[...].astype(BF16)
        o_ref[...] = (jnp.dot(mixed, wo_ref[...], preferred_element_type=F32)
                      + bo_ref[...] + x_ref[...])


def _fnet_seq(ac, asn, p, q, wo, bo, x2d, batch, seq):
    d = x2d.shape[1]
    mt, kt = seq // DFT_TM, seq // DFT_TK
    return pl.pallas_call(
        _fnet_seq_kernel,
        grid=(batch, mt, kt),
        in_specs=[
            pl.BlockSpec((DFT_TM, DFT_TK), lambda b, m, k: (m, k)),
            pl.BlockSpec((DFT_TM, DFT_TK), lambda b, m, k: (m, k)),
            pl.BlockSpec((DFT_TK, d), lambda b, m, k: (b * kt + k, 0)),
            pl.BlockSpec((DFT_TK, d), lambda b, m, k: (b * kt + k, 0)),
            pl.BlockSpec((d, d), lambda b, m, k: (0, 0)),
            pl.BlockSpec((1, d), lambda b, m, k: (0, 0)),
            pl.BlockSpec((DFT_TM, d), lambda b, m, k: (b * mt + m, 0)),
        ],
        out_specs=pl.BlockSpec((DFT_TM, d), lambda b, m, k: (b * mt + m, 0)),
        out_shape=jax.ShapeDtypeStruct(x2d.shape, F32),
        scratch_shapes=[pltpu.VMEM((DFT_TM, d), F32)],
        compiler_params=_params(3),
        name="fnet_seq",
    )(ac, asn, p, q, wo, bo, x2d)


def _dft_tables(seq):
    n = jnp.arange(seq, dtype=I32)
    kn = (n[:, None] * n[None, :]) % seq
    ang = kn.astype(F32) * F32(2.0 * math.pi / seq)
    s_seq = F32(1.0 / math.sqrt(seq))
    ac = (jnp.cos(ang) * s_seq).astype(BF16)
    asn = (-jnp.sin(ang) * s_seq).astype(BF16)
    c = jnp.arange(FNET_GROUP_DIM, dtype=I32)
    cc = (c[:, None] * c[None, :]) % FNET_GROUP_DIM
    angc = cc.astype(F32) * F32(2.0 * math.pi / FNET_GROUP_DIM)
    s_ch = F32(1.0 / math.sqrt(FNET_GROUP_DIM))
    cs = jnp.concatenate([jnp.cos(angc) * s_ch, jnp.sin(angc) * s_ch], axis=1).astype(BF16)
    return ac, asn, cs


def _fourier_layer(x2d, g, w_o, b_o, batch, seq):
    ac, asn, cs = _dft_tables(seq)
    p, q = _fnet_chan(x2d, g.reshape(1, -1), cs)
    return _fnet_seq(ac, asn, p, q, w_o.astype(BF16), b_o.reshape(1, -1), x2d, batch, seq)


def _qkv_kernel(x_ref, g_ref, w_ref, b_ref, cos_ref, sin_ref, q_ref, k_ref, v_ref):
    h = _rms(x_ref[...], g_ref[...]).astype(BF16)
    qkv = jnp.dot(h, w_ref[...], preferred_element_type=F32) + b_ref[...]
    cos = cos_ref[...]
    sin = sin_ref[...]
    lane = lax.broadcasted_iota(I32, cos.shape, 1)
    first_half = (lane % HEAD_DIM) < (HEAD_DIM // 2)
    n_rope = (N_HEADS + N_KV_HEADS) * HEAD_DIM // LANES
    n_q = N_HEADS * HEAD_DIM // LANES
    for c in range(n_rope):
        xc = qkv[:, c * LANES:(c + 1) * LANES]
        fwd = pltpu.roll(xc, LANES - HEAD_DIM // 2, axis=1)
        bwd = pltpu.roll(xc, HEAD_DIM // 2, axis=1)
        r = xc * cos + jnp.where(first_half, fwd, bwd) * sin
        if c < n_q:
            q_ref[:, c * LANES:(c + 1) * LANES] = (r * F32(HEAD_DIM ** -0.5)).astype(BF16)
        else:
            k_ref[:, (c - n_q) * LANES:(c - n_q + 1) * LANES] = r.astype(BF16)
    v_ref[...] = qkv[:, (N_HEADS + N_KV_HEADS) * HEAD_DIM:].astype(BF16)


def _qkv_rope(x2d, g, w_qkv, b_qkv, seq):
    t, d = x2d.shape
    qd = N_HEADS * HEAD_DIM
    pos = jnp.arange(seq, dtype=F32)
    inv_freq = ROPE_THETA ** (-jnp.arange(0, HEAD_DIM, 2, dtype=F32) / HEAD_DIM)
    ang = pos[:, None] * inv_freq[None, :]
    cos = jnp.cos(ang)
    sin = jnp.sin(ang)
    reps = LANES // (HEAD_DIM // 2)
    cos_l = jnp.tile(cos, (1, reps))
    sin_l = jnp.tile(jnp.concatenate([-sin, sin], axis=1), (1, LANES // HEAD_DIM))
    spt = seq // ROW_TILE
    return pl.pallas_call(
        _qkv_kernel,
        grid=(t // ROW_TILE,),
        in_specs=[
            pl.BlockSpec((ROW_TILE, d), lambda i: (i, 0)),
            pl.BlockSpec((1, d), lambda i: (0, 0)),
            pl.BlockSpec(w_qkv.shape, lambda i: (0, 0)),
            pl.BlockSpec((1, w_qkv.shape[1]), lambda i: (0, 0)),
            pl.BlockSpec((ROW_TILE, LANES), lambda i: (i % spt, 0)),
            pl.BlockSpec((ROW_TILE, LANES), lambda i: (i % spt, 0)),
        ],
        out_specs=[
            pl.BlockSpec((ROW_TILE, qd), lambda i: (i, 0)),
            pl.BlockSpec((ROW_TILE, KV_DIM), lambda i: (i, 0)),
            pl.BlockSpec((ROW_TILE, KV_DIM), lambda i: (i, 0)),
        ],
        out_shape=[
            jax.ShapeDtypeStruct((t, qd), BF16),
            jax.ShapeDtypeStruct((t, KV_DIM), BF16),
            jax.ShapeDtypeStruct((t, KV_DIM), BF16),
        ],
        compiler_params=_params(1),
        name="qkv_rope",
    )(x2d, g.reshape(1, -1), w_qkv.astype(BF16), b_qkv.reshape(1, -1), cos_l, sin_l)


def _attn_kernel(sink_ref, q_ref, kp_ref, kc_ref, kn_ref, vp_ref, vc_ref, vn_ref, o_ref, *, n_blocks):
    j = pl.program_id(1)
    blk = ATTN_BLOCK
    kcat = jnp.concatenate([kp_ref[...], kc_ref[...], kn_ref[...]], axis=0)
    vcat = jnp.concatenate([vp_ref[...], vc_ref[...], vn_ref[...]], axis=0)
    iq = lax.broadcasted_iota(I32, (blk, 3 * blk), 0)
    ik = lax.broadcasted_iota(I32, (blk, 3 * blk), 1)
    kpos = (j - 1) * blk + ik
    mask = (jnp.abs(iq + blk - ik) <= WINDOW) & (kpos >= 0) & (kpos < n_blocks * blk)
    mask = jnp.concatenate([mask] * Q_PER_KV, axis=0)
    q = q_ref[...]
    for g in range(N_KV_HEADS):
        kg = kcat[:, g * HEAD_DIM:(g + 1) * HEAD_DIM]
        vg = vcat[:, g * HEAD_DIM:(g + 1) * HEAD_DIM]
        heads = [g * Q_PER_KV + u for u in range(Q_PER_KV)]
        qg = jnp.concatenate([q[:, h * HEAD_DIM:(h + 1) * HEAD_DIM] for h in heads], axis=0)
        s = lax.dot_general(qg, kg, (((1,), (1,)), ((), ())), preferred_element_type=F32)
        s = jnp.where(mask, s, NEG_INF)
        sink = jnp.concatenate([jnp.full((blk, 1), sink_ref[h], F32) for h in heads], axis=0)
        mx = jnp.maximum(jnp.max(s, axis=-1, keepdims=True), sink)
        p = jnp.exp(s - mx)
        denom = jnp.sum(p, axis=-1, keepdims=True) + jnp.exp(sink - mx)
        probs = (p / denom).astype(BF16)
        o = jnp.dot(probs, vg, preferred_element_type=F32)
        for u, h in enumerate(heads):
            o_ref[:, h * HEAD_DIM:(h + 1) * HEAD_DIM] = o[u * blk:(u + 1) * blk].astype(BF16)


def _attention(q, k, v, sink, batch, seq):
    blk = ATTN_BLOCK
    nb = seq // blk
    qd = q.shape[1]

    def kv_spec(off):
        return pl.BlockSpec((blk, KV_DIM), lambda b, j: (b * nb + jnp.clip(j + off, 0, nb - 1), 0))

    return pl.pallas_call(
        functools.partial(_attn_kernel, n_blocks=nb),
        grid=(batch, nb),
        in_specs=[
            pl.BlockSpec(memory_space=pltpu.SMEM),
            pl.BlockSpec((blk, qd), lambda b, j: (b * nb + j, 0)),
            kv_spec(-1), kv_spec(0), kv_spec(1),
            kv_spec(-1), kv_spec(0), kv_spec(1),
        ],
        out_specs=pl.BlockSpec((blk, qd), lambda b, j: (b * nb + j, 0)),
        out_shape=jax.ShapeDtypeStruct(q.shape, BF16),
        compiler_params=_params(2),
        name="band_attn",
    )(sink.astype(F32), q, k, k, k, v, v, v)


def _proj_res_kernel(a_ref, w_ref, b_ref, x_ref, o_ref):
    o_ref[...] = (jnp.dot(a_ref[...], w_ref[...], preferred_element_type=F32)
                  + b_ref[...] + x_ref[...])


def _proj_residual(a, w, b, x2d):
    t, d = x2d.shape
    return pl.pallas_call(
        _proj_res_kernel,
        grid=(t // ROW_TILE,),
        in_specs=[
            pl.BlockSpec((ROW_TILE, a.shape[1]), lambda i: (i, 0)),
            pl.BlockSpec(w.shape, lambda i: (0, 0)),
            pl.BlockSpec((1, d), lambda i: (0, 0)),
            pl.BlockSpec((ROW_TILE, d), lambda i: (i, 0)),
        ],
        out_specs=pl.BlockSpec((ROW_TILE, d), lambda i: (i, 0)),
        out_shape=jax.ShapeDtypeStruct(x2d.shape, F32),
        compiler_params=_params(1),
        name="proj_residual",
    )(a, w.astype(BF16), b.reshape(1, -1), x2d)


def _attention_layer(x2d, g, w_qkv, b_qkv, sink, w_o, b_o, batch, seq):
    q, k, v = _qkv_rope(x2d, g, w_qkv, b_qkv, seq)
    o = _attention(q, k, v, sink, batch, seq)
    return _proj_residual(o, w_o, b_o, x2d)


def _nt_dot(a, b):
    return lax.dot_general(a, b, (((1,), (1,)), ((), ())), preferred_element_type=F32)


def _router_kernel(x_ref, g_ref, whi_ref, wlo_ref, rb_ref, ut_ref,
                   h_ref, e_ref, gate_ref, rank_ref, cnt_ref, carry_ref):
    i = pl.program_id(0)

    @pl.when(i == 0)
    def _():
        carry_ref[...] = jnp.zeros_like(carry_ref)

    h = _rms(x_ref[...], g_ref[...])
    h_ref[...] = h
    h_hi = h.astype(BF16)
    h_lo = (h - h_hi.astype(F32)).astype(BF16)
    logits = (_nt_dot(whi_ref[...], h_hi) + _nt_dot(whi_ref[...], h_lo)
              + _nt_dot(wlo_ref[...], h_hi) + rb_ref[...])
    n_e, tm = logits.shape
    eid = lax.broadcasted_iota(I32, (n_e, tm), 0)
    vals = logits
    top_v, top_i, hots = [], [], []
    for _ in range(TOP_K):
        mx = jnp.max(vals, axis=0, keepdims=True)
        idx = jnp.min(jnp.where(vals == mx, eid, n_e), axis=0, keepdims=True)
        hot = eid == idx
        vals = jnp.where(hot, -jnp.inf, vals)
        top_v.append(mx)
        top_i.append(idx)
        hots.append(hot)
    ex = [jnp.exp(v - top_v[0]) for v in top_v]
    den = ex[0] + ex[1] + ex[2] + ex[3]
    member = (hots[0] | hots[1] | hots[2] | hots[3]).astype(F32)
    cum = jnp.dot(member.astype(BF16), ut_ref[...], preferred_element_type=F32)
    carry = carry_ref[:, 0:1]
    excl = carry + cum - member
    for kk in range(TOP_K):
        e_ref[kk:kk + 1, :] = top_i[kk]
        gate_ref[kk:kk + 1, :] = ex[kk] / den
        rank_ref[kk:kk + 1, :] = jnp.sum(jnp.where(hots[kk], excl, 0.0), axis=0, keepdims=True).astype(I32)
    new_carry = carry + jnp.sum(member, axis=1, keepdims=True)
    carry_ref[...] = jnp.broadcast_to(new_carry, carry_ref.shape)
    cnt_ref[...] = jnp.broadcast_to(new_carry, cnt_ref.shape)


def _router(x2d, g, router_w, router_b):
    t, d = x2d.shape
    tm = ROW_TILE
    wt = router_w.T.astype(F32)
    w_hi = wt.astype(BF16)
    w_lo = (wt - w_hi.astype(F32)).astype(BF16)
    ut = (jnp.arange(tm, dtype=I32)[:, None] <= jnp.arange(tm, dtype=I32)[None, :]).astype(BF16)
    kt_spec = pl.BlockSpec((TOP_K, tm), lambda i: (0, i))
    return pl.pallas_call(
        _router_kernel,
        grid=(t // tm,),
        in_specs=[
            pl.BlockSpec((tm, d), lambda i: (i, 0)),
            pl.BlockSpec((1, d), lambda i: (0, 0)),
            pl.BlockSpec((N_EXPERTS, d), lambda i: (0, 0)),
            pl.BlockSpec((N_EXPERTS, d), lambda i: (0, 0)),
            pl.BlockSpec((N_EXPERTS, 1), lambda i: (0, 0)),
            pl.BlockSpec((tm, tm), lambda i: (0, 0)),
        ],
        out_specs=[
            pl.BlockSpec((tm, d), lambda i: (i, 0)),
            kt_spec, kt_spec, kt_spec,
            pl.BlockSpec((N_EXPERTS, LANES), lambda i: (0, 0)),
        ],
        out_shape=[
            jax.ShapeDtypeStruct((t, d), F32),
            jax.ShapeDtypeStruct((TOP_K, t), I32),
            jax.ShapeDtypeStruct((TOP_K, t), F32),
            jax.ShapeDtypeStruct((TOP_K, t), I32),
            jax.ShapeDtypeStruct((N_EXPERTS, LANES), F32),
        ],
        scratch_shapes=[pltpu.VMEM((N_EXPERTS, LANES), F32)],
        compiler_params=_params(1),
        name="router",
    )(x2d, g.reshape(1, -1), w_hi, w_lo, router_b.reshape(-1, 1).astype(F32), ut)


def _expert_kernel(be_ref, nv_ref, idx_hbm, h_hbm, wgu_ref, bgu_ref, wd_ref, bd_ref, y_ref,
                   idx_smem, xbuf, wgu_bf, wd_bf, isem, gsem):
    i = pl.program_id(0)
    nv = nv_ref[0]
    bm = xbuf.shape[1]
    d_ff = wd_bf.shape[0]

    def idx_copy(blk, s):
        return pltpu.make_async_copy(idx_hbm.at[pl.ds(blk * bm, bm)], idx_smem.at[s], isem.at[s])

    @pl.when(i < nv)
    def _():
        s = lax.rem(i, 2)

        @pl.when(i == 0)
        def _():
            idx_copy(0, 0).start()

        idx_copy(i, s).wait()
        for r in range(bm):
            tok = idx_smem[s, r]
            pltpu.make_async_copy(h_hbm.at[pl.ds(tok, 1), :], xbuf.at[s, pl.ds(r, 1), :], gsem.at[s]).start()

        @pl.when(i + 1 < nv)
        def _():
            idx_copy(i + 1, 1 - s).start()

    c = i - 1
    live = (c >= 0) & (c < nv)

    @pl.when(live)
    def _():
        s = lax.rem(c, 2)
        prev = jnp.maximum(c - 1, 0)

        @pl.when((c == 0) | (be_ref[c] != be_ref[prev]))
        def _():
            wgu_bf[...] = wgu_ref[0].astype(BF16)
            wd_bf[...] = wd_ref[0].astype(BF16)

        pltpu.make_async_copy(h_hbm.at[pl.ds(0, bm), :], xbuf.at[s], gsem.at[s]).wait()
        x = xbuf[s].astype(BF16)
        gu = jnp.dot(x, wgu_bf[...], preferred_element_type=F32) + bgu_ref[0]
        gate = jnp.minimum(gu[:, :d_ff], SWIGLU_LIMIT)
        up = jnp.clip(gu[:, d_ff:], -SWIGLU_LIMIT, SWIGLU_LIMIT)
        glu = gate * jax.nn.sigmoid(gate * SWIGLU_ALPHA)
        act = ((up + 1.0) * glu).astype(BF16)
        y_ref[...] = jnp.dot(act, wd_bf[...], preferred_element_type=F32) + bd_ref[0]

    @pl.when(jnp.logical_not(live))
    def _():
        y_ref[...] = jnp.zeros_like(y_ref)


def _experts(h, slot_tok, block_expert, n_valid, w_gu, b_gu, w_down, b_down, n_blocks):
    t, d = h.shape
    bm = MOE_BM
    d_ff = w_down.shape[1]

    def cidx(i):
        return jnp.maximum(i - 1, 0)

    grid_spec = pltpu.PrefetchScalarGridSpec(
        num_scalar_prefetch=2,
        grid=(n_blocks + 1,),
        in_specs=[
            pl.BlockSpec(memory_space=pl.ANY),
            pl.BlockSpec(memory_space=pl.ANY),
            pl.BlockSpec((1, d, 2 * d_ff), lambda i, be, nv: (be[cidx(i)], 0, 0)),
            pl.BlockSpec((1, 1, 2 * d_ff), lambda i, be, nv: (be[cidx(i)], 0, 0)),
            pl.BlockSpec((1, d_ff, d), lambda i, be, nv: (be[cidx(i)], 0, 0)),
            pl.BlockSpec((1, 1, d), lambda i, be, nv: (be[cidx(i)], 0, 0)),
        ],
        out_specs=pl.BlockSpec((bm, d), lambda i, be, nv: (cidx(i), 0)),
        scratch_shapes=[
            pltpu.SMEM((2, bm), I32),
            pltpu.VMEM((2, bm, d), F32),
            pltpu.VMEM((d, 2 * d_ff), BF16),
            pltpu.VMEM((d_ff, d), BF16),
            pltpu.SemaphoreType.DMA((2,)),
            pltpu.SemaphoreType.DMA((2,)),
        ],
    )
    return pl.pallas_call(
        _expert_kernel,
        grid_spec=grid_spec,
        out_shape=jax.ShapeDtypeStruct((n_blocks * bm, d), F32),
        compiler_params=_params(1),
        name="moe_experts",
    )(block_expert, n_valid, slot_tok, h, w_gu, b_gu.reshape(N_EXPERTS, 1, -1), w_down,
      b_down.reshape(N_EXPERTS, 1, -1))


def _combine_kernel(didx_hbm, ys_hbm, x_ref, gt_ref, fg_ref, o_ref, idx_smem, buf, isem, gsem, *, final_norm):
    i = pl.program_id(0)
    n = pl.num_programs(0) - 1
    rows = buf.shape[1]
    tc = rows // TOP_K

    def idx_copy(blk, s):
        return pltpu.make_async_copy(didx_hbm.at[pl.ds(blk * rows, rows)], idx_smem.at[s], isem.at[s])

    @pl.when(i < n)
    def _():
        s = lax.rem(i, 2)

        @pl.when(i == 0)
        def _():
            idx_copy(0, 0).start()

        idx_copy(i, s).wait()
        for r in range(rows):
            src = idx_smem[s, r]
            pltpu.make_async_copy(ys_hbm.at[pl.ds(src, 1), :], buf.at[s, pl.ds(r, 1), :], gsem.at[s]).start()

        @pl.when(i + 1 < n)
        def _():
            idx_copy(i + 1, 1 - s).start()

    @pl.when(i >= 1)
    def _():
        s = lax.rem(i - 1, 2)
        pltpu.make_async_copy(ys_hbm.at[pl.ds(0, rows), :], buf.at[s], gsem.at[s]).wait()
        acc = x_ref[...]
        gt = gt_ref[...]
        for kk in range(TOP_K):
            acc = acc + gt[:, kk:kk + 1] * buf[s, kk * tc:(kk + 1) * tc, :]
        if final_norm:
            acc = _rms(acc, fg_ref[...])
        o_ref[...] = acc


def _combine(ys, dest_tiles, gates_t, x2d, final_g):
    t, d = x2d.shape
    tc = COMBINE_TILE
    n = t // tc
    rows = TOP_K * tc

    def cidx(i):
        return jnp.maximum(i - 1, 0)

    fg = jnp.ones((1, d), F32) if final_g is None else final_g.reshape(1, -1).astype(F32)
    return pl.pallas_call(
        functools.partial(_combine_kernel, final_norm=final_g is not None),
        grid=(n + 1,),
        in_specs=[
            pl.BlockSpec(memory_space=pl.ANY),
            pl.BlockSpec(memory_space=pl.ANY),
            pl.BlockSpec((tc, d), lambda i: (cidx(i), 0)),
            pl.BlockSpec((tc, TOP_K), lambda i: (cidx(i), 0)),
            pl.BlockSpec((1, d), lambda i: (0, 0)),
        ],
        out_specs=pl.BlockSpec((tc, d), lambda i: (cidx(i), 0)),
        out_shape=jax.ShapeDtypeStruct((t, d), F32),
        scratch_shapes=[
            pltpu.SMEM((2, rows), I32),
            pltpu.VMEM((2, rows, d), F32),
            pltpu.SemaphoreType.DMA((2,)),
            pltpu.SemaphoreType.DMA((2,)),
        ],
        compiler_params=_params(1),
        name="moe_combine",
    )(dest_tiles, ys, x2d, gates_t, fg)


def _moe_layer(x2d, g, router_w, router_b, w_gu, b_gu, w_down, b_down, final_g):
    t, d = x2d.shape
    bm = MOE_BM
    a = t * TOP_K
    n_blocks = a // bm + N_EXPERTS

    h, e_kt, gate_kt, rank_kt, cnt = _router(x2d, g, router_w, router_b)

    counts = cnt[:, 0].astype(I32)
    blocks_e = (counts + bm - 1) // bm
    blk_end = jnp.cumsum(blocks_e)
    blk_start = blk_end - blocks_e
    n_valid = blk_end[-1]
    tok_start = jnp.cumsum(counts) - counts

    flat_e = e_kt.T.reshape(a)
    order = jnp.argsort(flat_e, stable=True)
    sorted_tok = (order // TOP_K).astype(I32)

    bid = jnp.arange(n_blocks, dtype=I32)
    be = jnp.minimum(jnp.searchsorted(blk_end, bid, side='right'), N_EXPERTS - 1).astype(I32)
    last_e = be[jnp.maximum(n_valid - 1, 0)]
    block_expert = jnp.where(bid < n_valid, be, last_e)
    within = (bid - blk_start[be])[:, None] * bm + jnp.arange(bm, dtype=I32)[None, :]
    row_ok = (bid < n_valid)[:, None] & (within < counts[be][:, None])
    src = jnp.clip(tok_start[be][:, None] + within, 0, a - 1)
    slot_tok = jnp.where(row_ok, sorted_tok[src], 0).reshape(-1).astype(I32)

    ys = _experts(h, slot_tok, block_expert, n_valid.reshape(1).astype(I32),
                  w_gu, b_gu, w_down, b_down, n_blocks)

    dest = (blk_start * bm)[e_kt] + rank_kt
    tc = COMBINE_TILE
    dest_tiles = dest.reshape(TOP_K, t // tc, tc).transpose(1, 0, 2).reshape(-1).astype(I32)
    return _combine(ys, dest_tiles, gate_kt.T, x2d, final_g)


def kernel(x, norm_mix_g, norm_ffn_g, fnet_w_o, fnet_b_o, attn_w_qkv, attn_b_qkv, attn_sink, attn_w_o, attn_b_o, router_w, router_b, expert_w_gu, expert_b_gu, expert_w_down, expert_b_down, final_norm_g):
    batch, seq, d = x.shape
    depth = norm_mix_g.shape[0]
    x2d = x.reshape(batch * seq, d)
    for i in range(depth):
        j = i // 2
        if i % 2 == 0:
            x2d = _fourier_layer(x2d, norm_mix_g[i], fnet_w_o[j], fnet_b_o[j], batch, seq)
        else:
            x2d = _attention_layer(x2d, norm_mix_g[i], attn_w_qkv[j], attn_b_qkv[j], attn_sink[j],
                                   attn_w_o[j], attn_b_o[j], batch, seq)
        fg = final_norm_g if i == depth - 1 else None
        x2d = _moe_layer(x2d, norm_ffn_g[i], router_w[i], router_b[i], expert_w_gu[i], expert_b_gu[i],
                         expert_w_down[i], expert_b_down[i], fg)
    return x2d.reshape(batch, seq, d)
```

```python
import functools
import math

import jax
import jax.numpy as jnp
from jax import lax
from jax.experimental import pallas as pl
from jax.experimental.pallas import tpu as pltpu

F32 = jnp.float32
BF16 = jnp.bfloat16
I32 = jnp.int32

D_MODEL = 1024
FNET_GROUPS = 8
FNET_GROUP_DIM = D_MODEL // FNET_GROUPS
HEAD_DIM = 64
N_HEADS = 16
N_KV_HEADS = 4
Q_PER_KV = N_HEADS // N_KV_HEADS
KV_DIM = N_KV_HEADS * HEAD_DIM
WINDOW = 128
ATTN_BLOCK = 128
ROPE_THETA = 10000.0
N_EXPERTS = 32
TOP_K = 4
D_FF = D_MODEL
SWIGLU_ALPHA = 1.702
SWIGLU_LIMIT = 7.0
RMS_EPS = 1e-5
NEG_INF = -1e30

LANES = 128
ROW_CHUNKS = D_MODEL // LANES
VMEM_LIMIT = 56 * 1024 * 1024

ROW_TILE = 512
MOE_BM = 256
COMBINE_TILE = 128
DFT_TM = 1024
DFT_TK = 512

_ARB = pltpu.ARBITRARY


def _params(n_axes):
    return pltpu.CompilerParams(dimension_semantics=(_ARB,) * n_axes, vmem_limit_bytes=VMEM_LIMIT)


def _nt_dot(a, b):
    return lax.dot_general(a, b, (((1,), (1,)), ((), ())), preferred_element_type=F32)


def _rms(x, g):
    ms = jnp.mean(x * x, axis=-1, keepdims=True)
    return (x * lax.rsqrt(ms + RMS_EPS)) * g


def _fnet_chan_kernel(x_ref, g_ref, cs_ref, p_ref, q_ref):
    h = _rms(x_ref[...], g_ref[...]).astype(BF16)
    for gi in range(FNET_GROUPS):
        lo = gi * FNET_GROUP_DIM
        pq = jnp.dot(h[:, lo:lo + FNET_GROUP_DIM], cs_ref[...], preferred_element_type=F32)
        p_ref[:, lo:lo + FNET_GROUP_DIM] = pq[:, :FNET_GROUP_DIM].astype(BF16)
        q_ref[:, lo:lo + FNET_GROUP_DIM] = pq[:, FNET_GROUP_DIM:].astype(BF16)


def _fnet_chan(x2d, g, cs):
    t, d = x2d.shape
    return pl.pallas_call(
        _fnet_chan_kernel,
        grid=(t // ROW_TILE,),
        in_specs=[
            pl.BlockSpec((ROW_TILE, d), lambda i: (i, 0)),
            pl.BlockSpec((1, d), lambda i: (0, 0)),
            pl.BlockSpec(cs.shape, lambda i: (0, 0)),
        ],
        out_specs=[pl.BlockSpec((ROW_TILE, d), lambda i: (i, 0))] * 2,
        out_shape=[jax.ShapeDtypeStruct((t, d), BF16)] * 2,
        compiler_params=_params(1),
        name="fnet_chan",
    )(x2d, g, cs)


def _fnet_seq_kernel(ac_ref, as_ref, p_ref, q_ref, wo_ref, bo_ref, x_ref, o_ref, acc_ref):
    k = pl.program_id(2)

    @pl.when(k == 0)
    def _():
        acc_ref[...] = jnp.zeros_like(acc_ref)

    acc_ref[...] += (jnp.dot(ac_ref[...], p_ref[...], preferred_element_type=F32)
                     + jnp.dot(as_ref[...], q_ref[...], preferred_element_type=F32))

    @pl.when(k == pl.num_programs(2) - 1)
    def _():
        mixed = acc_ref[...].astype(BF16)
        o_ref[...] = (jnp.dot(mixed, wo_ref[...], preferred_element_type=F32)
                      + bo_ref[...] + x_ref[...])


def _fnet_seq(ac, asn, p, q, wo, bo, x2d, batch, seq):
    d = x2d.shape[1]
    mt, kt = seq // DFT_TM, seq // DFT_TK
    return pl.pallas_call(
        _fnet_seq_kernel,
        grid=(batch, mt, kt),
        in_specs=[
            pl.BlockSpec((DFT_TM, DFT_TK), lambda b, m, k: (m, k)),
            pl.BlockSpec((DFT_TM, DFT_TK), lambda b, m, k: (m, k)),
            pl.BlockSpec((DFT_TK, d), lambda b, m, k: (b * kt + k, 0)),
            pl.BlockSpec((DFT_TK, d), lambda b, m, k: (b * kt + k, 0)),
            pl.BlockSpec((d, d), lambda b, m, k: (0, 0)),
            pl.BlockSpec((1, d), lambda b, m, k: (0, 0)),
            pl.BlockSpec((DFT_TM, d), lambda b, m, k: (b * mt + m, 0)),
        ],
        out_specs=pl.BlockSpec((DFT_TM, d), lambda b, m, k: (b * mt + m, 0)),
        out_shape=jax.ShapeDtypeStruct(x2d.shape, F32),
        scratch_shapes=[pltpu.VMEM((DFT_TM, d), F32)],
        compiler_params=_params(3),
        name="fnet_seq",
    )(ac, asn, p, q, wo, bo, x2d)


def _dft_tables(seq):
    n = jnp.arange(seq, dtype=I32)
    kn = (n[:, None] * n[None, :]) % seq
    ang = kn.astype(F32) * F32(2.0 * math.pi / seq)
    s_seq = F32(1.0 / math.sqrt(seq))
    ac = (jnp.cos(ang) * s_seq).astype(BF16)
    asn = (-jnp.sin(ang) * s_seq).astype(BF16)
    c = jnp.arange(FNET_GROUP_DIM, dtype=I32)
    cc = (c[:, None] * c[None, :]) % FNET_GROUP_DIM
    angc = cc.astype(F32) * F32(2.0 * math.pi / FNET_GROUP_DIM)
    s_ch = F32(1.0 / math.sqrt(FNET_GROUP_DIM))
    cs = jnp.concatenate([jnp.cos(angc) * s_ch, jnp.sin(angc) * s_ch], axis=1).astype(BF16)
    return ac, asn, cs


def _fourier_layer(x2d, g, w_o, b_o, batch, seq):
    ac, asn, cs = _dft_tables(seq)
    p, q = _fnet_chan(x2d, g.reshape(1, -1), cs)
    return _fnet_seq(ac, asn, p, q, w_o.astype(BF16), b_o.reshape(1, -1), x2d, batch, seq)


def _dup_halves(x, lo_half):
    swapped = pltpu.roll(x, HEAD_DIM, axis=1)
    return jnp.where(lo_half, x, swapped), jnp.where(lo_half, swapped, x)


def _qkv_kernel(x_ref, g_ref, w_ref, b_ref, cos_ref, sin_ref, q_ref, k_ref, v_ref):
    h = _rms(x_ref[...], g_ref[...]).astype(BF16)
    qkv = jnp.dot(h, w_ref[...], preferred_element_type=F32) + b_ref[...]
    cos = cos_ref[...]
    sin = sin_ref[...]
    lane = lax.broadcasted_iota(I32, cos.shape, 1)
    first_half = (lane % HEAD_DIM) < (HEAD_DIM // 2)
    lo_half = lane < HEAD_DIM
    n_q = N_HEADS * HEAD_DIM // LANES
    n_kv = KV_DIM // LANES

    def rope(xc):
        fwd = pltpu.roll(xc, LANES - HEAD_DIM // 2, axis=1)
        bwd = pltpu.roll(xc, HEAD_DIM // 2, axis=1)
        return xc * cos + jnp.where(first_half, fwd, bwd) * sin

    for c in range(n_q):
        r = rope(qkv[:, c * LANES:(c + 1) * LANES])
        q_ref[:, c * LANES:(c + 1) * LANES] = (r * F32(HEAD_DIM ** -0.5)).astype(BF16)
    for c in range(n_kv):
        kc = rope(qkv[:, (n_q + c) * LANES:(n_q + c + 1) * LANES])
        vc = qkv[:, (n_q + n_kv + c) * LANES:(n_q + n_kv + c + 1) * LANES]
        for ref, val in ((k_ref, kc), (v_ref, vc)):
            a, b = _dup_halves(val, lo_half)
            ref[:, (2 * c) * LANES:(2 * c + 1) * LANES] = a.astype(BF16)
            ref[:, (2 * c + 1) * LANES:(2 * c + 2) * LANES] = b.astype(BF16)


def _qkv_rope(x2d, g, w_qkv, b_qkv, seq):
    t, d = x2d.shape
    qd = N_HEADS * HEAD_DIM
    kvd = N_KV_HEADS * LANES
    pos = jnp.arange(seq, dtype=F32)
    inv_freq = ROPE_THETA ** (-jnp.arange(0, HEAD_DIM, 2, dtype=F32) / HEAD_DIM)
    ang = pos[:, None] * inv_freq[None, :]
    cos = jnp.cos(ang)
    sin = jnp.sin(ang)
    reps = LANES // (HEAD_DIM // 2)
    cos_l = jnp.tile(cos, (1, reps))
    sin_l = jnp.tile(jnp.concatenate([-sin, sin], axis=1), (1, LANES // HEAD_DIM))
    spt = seq // ROW_TILE
    return pl.pallas_call(
        _qkv_kernel,
        grid=(t // ROW_TILE,),
        in_specs=[
            pl.BlockSpec((ROW_TILE, d), lambda i: (i, 0)),
            pl.BlockSpec((1, d), lambda i: (0, 0)),
            pl.BlockSpec(w_qkv.shape, lambda i: (0, 0)),
            pl.BlockSpec((1, w_qkv.shape[1]), lambda i: (0, 0)),
            pl.BlockSpec((ROW_TILE, LANES), lambda i: (i % spt, 0)),
            pl.BlockSpec((ROW_TILE, LANES), lambda i: (i % spt, 0)),
        ],
        out_specs=[
            pl.BlockSpec((ROW_TILE, qd), lambda i: (i, 0)),
            pl.BlockSpec((ROW_TILE, kvd), lambda i: (i, 0)),
            pl.BlockSpec((ROW_TILE, kvd), lambda i: (i, 0)),
        ],
        out_shape=[
            jax.ShapeDtypeStruct((t, qd), BF16),
            jax.ShapeDtypeStruct((t, kvd), BF16),
            jax.ShapeDtypeStruct((t, kvd), BF16),
        ],
        compiler_params=_params(1),
        name="qkv_rope",
    )(x2d, g.reshape(1, -1), w_qkv.astype(BF16), b_qkv.reshape(1, -1), cos_l, sin_l)


def _attn_kernel(sink_ref, q_ref, kp_ref, kc_ref, kn_ref, vp_ref, vc_ref, vn_ref, o_ref, *, n_blocks):
    j = pl.program_id(1)
    blk = ATTN_BLOCK
    kcat = jnp.concatenate([kp_ref[...], kc_ref[...], kn_ref[...]], axis=0)
    vcat = jnp.concatenate([vp_ref[...], vc_ref[...], vn_ref[...]], axis=0)
    iq = lax.broadcasted_iota(I32, (blk, 3 * blk), 0)
    ik = lax.broadcasted_iota(I32, (blk, 3 * blk), 1)
    kpos = (j - 1) * blk + ik
    mask = (jnp.abs(iq + blk - ik) <= WINDOW) & (kpos >= 0) & (kpos < n_blocks * blk)
    mask = jnp.concatenate([mask] * Q_PER_KV, axis=0)
    lo_half = lax.broadcasted_iota(I32, (1, LANES), 1) < HEAD_DIM
    ones = jnp.ones((3 * blk, LANES), BF16)
    zero = jnp.zeros((), BF16)
    half = Q_PER_KV // 2 * blk
    for g in range(N_KV_HEADS):
        kd = kcat[:, g * LANES:(g + 1) * LANES]
        vd = vcat[:, g * LANES:(g + 1) * LANES]
        c0 = g * Q_PER_KV // 2
        qa = q_ref[:, c0 * LANES:(c0 + 1) * LANES]
        qb = q_ref[:, (c0 + 1) * LANES:(c0 + 2) * LANES]
        order = (0, 2, 1, 3)
        qs = jnp.concatenate([jnp.where(lo_half, qa, zero), jnp.where(lo_half, qb, zero),
                              jnp.where(lo_half, zero, qa), jnp.where(lo_half, zero, qb)], axis=0)
        s = _nt_dot(qs, kd)
        s = jnp.where(mask, s, NEG_INF)
        sink = jnp.concatenate([jnp.full((blk, 1), sink_ref[g * Q_PER_KV + u], F32) for u in order], axis=0)
        mx = jnp.maximum(jnp.max(s, axis=-1, keepdims=True), sink)
        p = jnp.exp(s - mx).astype(BF16)
        esink = jnp.exp(sink - mx)
        v_lo = jnp.concatenate([jnp.where(lo_half, vd, zero), ones], axis=1)
        v_hi = jnp.concatenate([jnp.where(lo_half, zero, vd), ones], axis=1)
        r_lo = jnp.dot(p[:half], v_lo, preferred_element_type=F32)
        r_hi = jnp.dot(p[half:], v_hi, preferred_element_type=F32)
        n_lo = r_lo[:, :LANES] / (r_lo[:, LANES:] + esink[:half])
        n_hi = r_hi[:, :LANES] / (r_hi[:, LANES:] + esink[half:])
        o_ref[:, c0 * LANES:(c0 + 1) * LANES] = (n_lo[:blk] + n_hi[:blk]).astype(BF16)
        o_ref[:, (c0 + 1) * LANES:(c0 + 2) * LANES] = (n_lo[blk:] + n_hi[blk:]).astype(BF16)


def _attention(q, k, v, sink, batch, seq):
    blk = ATTN_BLOCK
    nb = seq // blk
    qd = q.shape[1]
    kvd = k.shape[1]

    def kv_spec(off):
        return pl.BlockSpec((blk, kvd), lambda b, j: (b * nb + jnp.clip(j + off, 0, nb - 1), 0))

    return pl.pallas_call(
        functools.partial(_attn_kernel, n_blocks=nb),
        grid=(batch, nb),
        in_specs=[
            pl.BlockSpec(memory_space=pltpu.SMEM),
            pl.BlockSpec((blk, qd), lambda b, j: (b * nb + j, 0)),
            kv_spec(-1), kv_spec(0), kv_spec(1),
            kv_spec(-1), kv_spec(0), kv_spec(1),
        ],
        out_specs=pl.BlockSpec((blk, qd), lambda b, j: (b * nb + j, 0)),
        out_shape=jax.ShapeDtypeStruct(q.shape, BF16),
        compiler_params=_params(2),
        name="band_attn",
    )(sink.astype(F32), q, k, k, k, v, v, v)


def _proj_res_kernel(a_ref, w_ref, b_ref, x_ref, o_ref):
    o_ref[...] = (jnp.dot(a_ref[...], w_ref[...], preferred_element_type=F32)
                  + b_ref[...] + x_ref[...])


def _proj_residual(a, w, b, x2d):
    t, d = x2d.shape
    return pl.pallas_call(
        _proj_res_kernel,
        grid=(t // ROW_TILE,),
        in_specs=[
            pl.BlockSpec((ROW_TILE, a.shape[1]), lambda i: (i, 0)),
            pl.BlockSpec(w.shape, lambda i: (0, 0)),
            pl.BlockSpec((1, d), lambda i: (0, 0)),
            pl.BlockSpec((ROW_TILE, d), lambda i: (i, 0)),
        ],
        out_specs=pl.BlockSpec((ROW_TILE, d), lambda i: (i, 0)),
        out_shape=jax.ShapeDtypeStruct(x2d.shape, F32),
        compiler_params=_params(1),
        name="proj_residual",
    )(a, w.astype(BF16), b.reshape(1, -1), x2d)


def _attention_layer(x2d, g, w_qkv, b_qkv, sink, w_o, b_o, batch, seq):
    q, k, v = _qkv_rope(x2d, g, w_qkv, b_qkv, seq)
    o = _attention(q, k, v, sink, batch, seq)
    return _proj_residual(o, w_o, b_o, x2d)


def _router_kernel(x_ref, g_ref, whi_ref, wlo_ref, rb_ref, ut_ref,
                   h_ref, e_ref, gate_ref, rank_ref, cnt_ref, carry_ref):
    i = pl.program_id(0)

    @pl.when(i == 0)
    def _():
        carry_ref[...] = jnp.zeros_like(carry_ref)

    h = _rms(x_ref[...], g_ref[...])
    for ch in range(ROW_CHUNKS):
        h_ref[pl.ds(ch, h.shape[0], stride=ROW_CHUNKS), :] = h[:, ch * LANES:(ch + 1) * LANES]
    h_hi = h.astype(BF16)
    h_lo = (h - h_hi.astype(F32)).astype(BF16)
    logits = (_nt_dot(whi_ref[...], h_hi) + _nt_dot(whi_ref[...], h_lo)
              + _nt_dot(wlo_ref[...], h_hi) + rb_ref[...])
    n_e, tm = logits.shape
    eid = lax.broadcasted_iota(I32, (n_e, tm), 0)
    vals = logits
    top_v, top_i, hots = [], [], []
    for _ in range(TOP_K):
        mx = jnp.max(vals, axis=0, keepdims=True)
        idx = jnp.min(jnp.where(vals == mx, eid, n_e), axis=0, keepdims=True)
        hot = eid == idx
        vals = jnp.where(hot, -jnp.inf, vals)
        top_v.append(mx)
        top_i.append(idx)
        hots.append(hot)
    ex = [jnp.exp(v - top_v[0]) for v in top_v]
    den = ex[0] + ex[1] + ex[2] + ex[3]
    member = (hots[0] | hots[1] | hots[2] | hots[3]).astype(F32)
    cum = jnp.dot(member.astype(BF16), ut_ref[...], preferred_element_type=F32)
    carry = carry_ref[:, 0:1]
    excl = carry + cum - member
    for kk in range(TOP_K):
        e_ref[kk:kk + 1, :] = top_i[kk]
        gate_ref[kk:kk + 1, :] = ex[kk] / den
        rank_ref[kk:kk + 1, :] = jnp.sum(jnp.where(hots[kk], excl, 0.0), axis=0, keepdims=True).astype(I32)
    new_carry = carry + jnp.sum(member, axis=1, keepdims=True)
    carry_ref[...] = jnp.broadcast_to(new_carry, carry_ref.shape)
    cnt_ref[...] = jnp.broadcast_to(new_carry, cnt_ref.shape)


def _router(x2d, g, router_w, router_b):
    t, d = x2d.shape
    tm = ROW_TILE
    wt = router_w.T.astype(F32)
    w_hi = wt.astype(BF16)
    w_lo = (wt - w_hi.astype(F32)).astype(BF16)
    ut = (jnp.arange(tm, dtype=I32)[:, None] <= jnp.arange(tm, dtype=I32)[None, :]).astype(BF16)
    kt_spec = pl.BlockSpec((TOP_K, tm), lambda i: (0, i))
    return pl.pallas_call(
        _router_kernel,
        grid=(t // tm,),
        in_specs=[
            pl.BlockSpec((tm, d), lambda i: (i, 0)),
            pl.BlockSpec((1, d), lambda i: (0, 0)),
            pl.BlockSpec((N_EXPERTS, d), lambda i: (0, 0)),
            pl.BlockSpec((N_EXPERTS, d), lambda i: (0, 0)),
            pl.BlockSpec((N_EXPERTS, 1), lambda i: (0, 0)),
            pl.BlockSpec((tm, tm), lambda i: (0, 0)),
        ],
        out_specs=[
            pl.BlockSpec((tm * ROW_CHUNKS, LANES), lambda i: (i, 0)),
            kt_spec, kt_spec, kt_spec,
            pl.BlockSpec((N_EXPERTS, LANES), lambda i: (0, 0)),
        ],
        out_shape=[
            jax.ShapeDtypeStruct((t * ROW_CHUNKS, LANES), F32),
            jax.ShapeDtypeStruct((TOP_K, t), I32),
            jax.ShapeDtypeStruct((TOP_K, t), F32),
            jax.ShapeDtypeStruct((TOP_K, t), I32),
            jax.ShapeDtypeStruct((N_EXPERTS, LANES), F32),
        ],
        scratch_shapes=[pltpu.VMEM((N_EXPERTS, LANES), F32)],
        compiler_params=_params(1),
        name="router",
    )(x2d, g.reshape(1, -1), w_hi, w_lo, router_b.reshape(-1, 1).astype(F32), ut)


def _expert_kernel(be_ref, base_ref, idx_hbm, h_hbm, wgu_ref, bgu_ref, wd_ref, bd_ref, y_ref,
                   idx_smem, xbuf, wgu_bf, wd_bf, isem, gsem, *, n_blocks):
    i = pl.program_id(0)
    last_blk = n_blocks - 1
    bm = y_ref.shape[0]
    d_ff = wd_bf.shape[0]
    win = idx_smem.shape[1]
    nch = ROW_CHUNKS
    c = jnp.maximum(i - 1, 0)

    def idx_copy(blk, s):
        start = pl.multiple_of((base_ref[blk] // LANES) * LANES, LANES)
        return pltpu.make_async_copy(idx_hbm.at[pl.ds(start, win)], idx_smem.at[s], isem.at[s])

    def rows_done(s):
        return pltpu.make_async_copy(h_hbm.at[pl.ds(0, bm * nch), :], xbuf.at[s], gsem.at[s])

    @pl.when(i == 0)
    def _():
        idx_copy(0, 0).start()
        rows_done(1).start()

    @pl.when((i <= 1) | (be_ref[c] != be_ref[jnp.maximum(c - 1, 0)]))
    def _():
        wgu_bf[...] = wgu_ref[0, 0].astype(BF16)
        wd_bf[...] = wd_ref[0, 0].astype(BF16)

    def step(s):
        blk = jnp.minimum(i, last_blk)
        idx_copy(blk, s).wait()
        rows_done(1 - s).wait()
        idx_copy(jnp.minimum(i + 1, last_blk), 1 - s).start()
        off = lax.rem(base_ref[blk], LANES)
        for r in range(bm):
            tok = idx_smem[s, off + r]
            src = h_hbm.at[pl.ds(pl.multiple_of(tok * nch, nch), nch), :]
            pltpu.make_async_copy(src, xbuf.at[s, pl.ds(r * nch, nch), :], gsem.at[s]).start()
        x = jnp.concatenate([xbuf[1 - s, pl.ds(ch, bm, stride=nch), :] for ch in range(nch)], axis=1)
        gu = jnp.dot(x.astype(BF16), wgu_bf[...], preferred_element_type=F32) + bgu_ref[0, 0]
        gate = jnp.minimum(gu[:, :d_ff], SWIGLU_LIMIT)
        up = jnp.clip(gu[:, d_ff:], -SWIGLU_LIMIT, SWIGLU_LIMIT)
        glu = gate * jax.nn.sigmoid(gate * SWIGLU_ALPHA)
        act = ((up + 1.0) * glu).astype(BF16)
        y_ref[...] = jnp.dot(act, wd_bf[...], preferred_element_type=F32) + bd_ref[0, 0]

    for s in range(2):
        pl.when(lax.rem(i, 2) == s)(functools.partial(step, s))

    @pl.when(i == n_blocks)
    def _():
        s_last = n_blocks % 2
        idx_copy(last_blk, 1 - s_last).wait()
        rows_done(s_last).wait()


def _experts(h8, sorted_tok, block_expert, block_base, layer, w_gu, b_gu, w_down, b_down, n_blocks):
    bm = MOE_BM
    depth, n_e, d_ff, d = w_down.shape

    def cidx(i):
        return jnp.maximum(i - 1, 0)

    def wmap(i, be, base):
        return (layer, be[cidx(i)], 0, 0)

    grid_spec = pltpu.PrefetchScalarGridSpec(
        num_scalar_prefetch=2,
        grid=(n_blocks + 1,),
        in_specs=[
            pl.BlockSpec(memory_space=pl.ANY),
            pl.BlockSpec(memory_space=pl.ANY),
            pl.BlockSpec((1, 1, d, 2 * d_ff), wmap),
            pl.BlockSpec((1, 1, 1, 2 * d_ff), wmap),
            pl.BlockSpec((1, 1, d_ff, d), wmap),
            pl.BlockSpec((1, 1, 1, d), wmap),
        ],
        out_specs=pl.BlockSpec((bm, d), lambda i, be, base: (cidx(i), 0)),
        scratch_shapes=[
            pltpu.SMEM((2, bm + LANES), I32),
            pltpu.VMEM((2, bm * ROW_CHUNKS, LANES), F32),
            pltpu.VMEM((d, 2 * d_ff), BF16),
            pltpu.VMEM((d_ff, d), BF16),
            pltpu.SemaphoreType.DMA((2,)),
            pltpu.SemaphoreType.DMA((2,)),
        ],
    )
    return pl.pallas_call(
        functools.partial(_expert_kernel, n_blocks=n_blocks),
        grid_spec=grid_spec,
        out_shape=jax.ShapeDtypeStruct((n_blocks * bm, d), F32),
        compiler_params=_params(1),
        name="moe_experts",
    )(block_expert, block_base, sorted_tok, h8, w_gu, b_gu.reshape(depth, n_e, 1, -1), w_down,
      b_down.reshape(depth, n_e, 1, -1))


def _combine_kernel(didx_hbm, ys_hbm, x_ref, gt_ref, fg_ref, o_ref, idx_smem, buf, isem, gsem, *, final_norm):
    i = pl.program_id(0)
    n = pl.num_programs(0) - 1
    rows = buf.shape[1]
    tc = rows // TOP_K

    def idx_copy(blk, s):
        return pltpu.make_async_copy(didx_hbm.at[pl.ds(blk * rows, rows)], idx_smem.at[s], isem.at[s])

    @pl.when(i < n)
    def _():
        s = lax.rem(i, 2)

        @pl.when(i == 0)
        def _():
            idx_copy(0, 0).start()

        idx_copy(i, s).wait()
        for r in range(rows):
            src = idx_smem[s, r]
            pltpu.make_async_copy(ys_hbm.at[pl.ds(src, 1), :], buf.at[s, pl.ds(r, 1), :], gsem.at[s]).start()

        @pl.when(i + 1 < n)
        def _():
            idx_copy(i + 1, 1 - s).start()

    @pl.when(i >= 1)
    def _():
        s = lax.rem(i - 1, 2)
        pltpu.make_async_copy(ys_hbm.at[pl.ds(0, rows), :], buf.at[s], gsem.at[s]).wait()
        acc = x_ref[...]
        gt = gt_ref[...]
        for kk in range(TOP_K):
            acc = acc + gt[:, kk:kk + 1] * buf[s, kk * tc:(kk + 1) * tc, :]
        if final_norm:
            acc = _rms(acc, fg_ref[...])
        o_ref[...] = acc


def _combine(ys, dest_tiles, gates_t, x2d, final_g):
    t, d = x2d.shape
    tc = COMBINE_TILE
    n = t // tc
    rows = TOP_K * tc

    def cidx(i):
        return jnp.maximum(i - 1, 0)

    fg = jnp.ones((1, d), F32) if final_g is None else final_g.reshape(1, -1).astype(F32)
    return pl.pallas_call(
        functools.partial(_combine_kernel, final_norm=final_g is not None),
        grid=(n + 1,),
        in_specs=[
            pl.BlockSpec(memory_space=pl.ANY),
            pl.BlockSpec(memory_space=pl.ANY),
            pl.BlockSpec((tc, d), lambda i: (cidx(i), 0)),
            pl.BlockSpec((tc, TOP_K), lambda i: (cidx(i), 0)),
            pl.BlockSpec((1, d), lambda i: (0, 0)),
        ],
        out_specs=pl.BlockSpec((tc, d), lambda i: (cidx(i), 0)),
        out_shape=jax.ShapeDtypeStruct((t, d), F32),
        scratch_shapes=[
            pltpu.SMEM((2, rows), I32),
            pltpu.VMEM((2, rows, d), F32),
            pltpu.SemaphoreType.DMA((2,)),
            pltpu.SemaphoreType.DMA((2,)),
        ],
        compiler_params=_params(1),
        name="moe_combine",
    )(dest_tiles, ys, x2d, gates_t, fg)


def _moe_layer(x2d, g, router_w, router_b, layer, w_gu, b_gu, w_down, b_down, final_g):
    t, d = x2d.shape
    bm = MOE_BM
    a = t * TOP_K
    n_blocks = a // bm + N_EXPERTS

    h, e_kt, gate_kt, rank_kt, cnt = _router(x2d, g, router_w, router_b)

    counts = cnt[:, 0].astype(I32)
    blocks_e = (counts + bm - 1) // bm
    blk_end = jnp.cumsum(blocks_e)
    blk_start = blk_end - blocks_e
    n_valid = blk_end[-1]
    tok_start = jnp.cumsum(counts) - counts

    a_id = jnp.arange(t, dtype=I32)[None, :] * TOP_K + jnp.arange(TOP_K, dtype=I32)[:, None]
    sorted_key = jnp.sort((e_kt * a + a_id).reshape(-1))
    sorted_tok = jnp.pad((sorted_key % a) // TOP_K, (0, bm + LANES))

    eids = jnp.arange(N_EXPERTS, dtype=I32)
    bid = jnp.arange(n_blocks, dtype=I32)
    be = jnp.minimum(jnp.sum((bid[:, None] >= blk_end[None, :]).astype(I32), axis=1), N_EXPERTS - 1)
    hot = be[:, None] == eids[None, :]
    base = (jnp.sum(jnp.where(hot, tok_start[None, :], 0), axis=1)
            + (bid - jnp.sum(jnp.where(hot, blk_start[None, :], 0), axis=1)) * bm)
    valid = bid < n_valid
    last_e = jnp.sum(jnp.where(bid == n_valid - 1, be, 0))
    block_expert = jnp.where(valid, be, last_e).astype(I32)
    block_base = jnp.where(valid, base, 0).astype(I32)

    ys = _experts(h, sorted_tok, block_expert, block_base, layer, w_gu, b_gu, w_down, b_down, n_blocks)

    dest = rank_kt
    for e in range(N_EXPERTS):
        dest = dest + jnp.where(e_kt == e, blk_start[e] * bm, 0)
    tc = COMBINE_TILE
    dest_tiles = dest.reshape(TOP_K, t // tc, tc).transpose(1, 0, 2).reshape(-1).astype(I32)
    return _combine(ys, dest_tiles, gate_kt.T, x2d, final_g)


def kernel(x, norm_mix_g, norm_ffn_g, fnet_w_o, fnet_b_o, attn_w_qkv, attn_b_qkv, attn_sink, attn_w_o, attn_b_o, router_w, router_b, expert_w_gu, expert_b_gu, expert_w_down, expert_b_down, final_norm_g):
    batch, seq, d = x.shape
    depth = norm_mix_g.shape[0]
    x2d = x.reshape(batch * seq, d)
    for i in range(depth):
        j = i // 2
        if i % 2 == 0:
            x2d = _fourier_layer(x2d, norm_mix_g[i], fnet_w_o[j], fnet_b_o[j], batch, seq)
        else:
            x2d = _attention_layer(x2d, norm_mix_g[i], attn_w_qkv[j], attn_b_qkv[j], attn_sink[j],
                                   attn_w_o[j], attn_b_o[j], batch, seq)
        fg = final_norm_g if i == depth - 1 else None
        x2d = _moe_layer(x2d, norm_ffn_g[i], router_w[i], router_b[i], i, expert_w_gu, expert_b_gu,
                         expert_w_down, expert_b_down, fg)
    return x2d.reshape(batch, seq, d)
```

```python
import functools
import math

import jax
import jax.numpy as jnp
from jax import lax
from jax.experimental import pallas as pl
from jax.experimental.pallas import tpu as pltpu

F32 = jnp.float32
BF16 = jnp.bfloat16
I32 = jnp.int32

D_MODEL = 1024
FNET_GROUPS = 8
FNET_GROUP_DIM = D_MODEL // FNET_GROUPS
HEAD_DIM = 64
N_HEADS = 16
N_KV_HEADS = 4
Q_PER_KV = N_HEADS // N_KV_HEADS
KV_DIM = N_KV_HEADS * HEAD_DIM
WINDOW = 128
ATTN_BLOCK = 128
ROPE_THETA = 10000.0
N_EXPERTS = 32
TOP_K = 4
D_FF = D_MODEL
SWIGLU_ALPHA = 1.702
SWIGLU_LIMIT = 7.0
RMS_EPS = 1e-5
NEG_INF = -1e30

LANES = 128
ROW_CHUNKS = D_MODEL // LANES
VMEM_LIMIT = 56 * 1024 * 1024

ROW_TILE = 512
MOE_BM = 256
COMBINE_TILE = 128
DFT_TM = 1024
DFT_TK = 512

_ARB = pltpu.ARBITRARY


def _params(n_axes):
    return pltpu.CompilerParams(dimension_semantics=(_ARB,) * n_axes, vmem_limit_bytes=VMEM_LIMIT)


def _nt_dot(a, b):
    return lax.dot_general(a, b, (((1,), (1,)), ((), ())), preferred_element_type=F32)


def _rms(x, g):
    ms = jnp.mean(x * x, axis=-1, keepdims=True)
    return (x * lax.rsqrt(ms + RMS_EPS)) * g


def _fnet_chan_kernel(x_ref, g_ref, cs_ref, p_ref, q_ref):
    h = _rms(x_ref[...], g_ref[...]).astype(BF16)
    for gi in range(FNET_GROUPS):
        lo = gi * FNET_GROUP_DIM
        pq = jnp.dot(h[:, lo:lo + FNET_GROUP_DIM], cs_ref[...], preferred_element_type=F32)
        p_ref[:, lo:lo + FNET_GROUP_DIM] = pq[:, :FNET_GROUP_DIM].astype(BF16)
        q_ref[:, lo:lo + FNET_GROUP_DIM] = pq[:, FNET_GROUP_DIM:].astype(BF16)


def _fnet_chan(x2d, g, cs):
    t, d = x2d.shape
    return pl.pallas_call(
        _fnet_chan_kernel,
        grid=(t // ROW_TILE,),
        in_specs=[
            pl.BlockSpec((ROW_TILE, d), lambda i: (i, 0)),
            pl.BlockSpec((1, d), lambda i: (0, 0)),
            pl.BlockSpec(cs.shape, lambda i: (0, 0)),
        ],
        out_specs=[pl.BlockSpec((ROW_TILE, d), lambda i: (i, 0))] * 2,
        out_shape=[jax.ShapeDtypeStruct((t, d), BF16)] * 2,
        compiler_params=_params(1),
        name="fnet_chan",
    )(x2d, g, cs)


def _fnet_seq_kernel(ac_ref, as_ref, p_ref, q_ref, wo_ref, bo_ref, x_ref, o_ref, acc_ref):
    k = pl.program_id(2)

    @pl.when(k == 0)
    def _():
        acc_ref[...] = jnp.zeros_like(acc_ref)

    acc_ref[...] += (jnp.dot(ac_ref[...], p_ref[...], preferred_element_type=F32)
                     + jnp.dot(as_ref[...], q_ref[...], preferred_element_type=F32))

    @pl.when(k == pl.num_programs(2) - 1)
    def _():
        mixed = acc_ref[...].astype(BF16)
        o_ref[...] = (jnp.dot(mixed, wo_ref[...], preferred_element_type=F32)
                      + bo_ref[...] + x_ref[...])


def _fnet_seq(ac, asn, p, q, wo, bo, x2d, batch, seq):
    d = x2d.shape[1]
    mt, kt = seq // DFT_TM, seq // DFT_TK
    return pl.pallas_call(
        _fnet_seq_kernel,
        grid=(batch, mt, kt),
        in_specs=[
            pl.BlockSpec((DFT_TM, DFT_TK), lambda b, m, k: (m, k)),
            pl.BlockSpec((DFT_TM, DFT_TK), lambda b, m, k: (m, k)),
            pl.BlockSpec((DFT_TK, d), lambda b, m, k: (b * kt + k, 0)),
            pl.BlockSpec((DFT_TK, d), lambda b, m, k: (b * kt + k, 0)),
            pl.BlockSpec((d, d), lambda b, m, k: (0, 0)),
            pl.BlockSpec((1, d), lambda b, m, k: (0, 0)),
            pl.BlockSpec((DFT_TM, d), lambda b, m, k: (b * mt + m, 0)),
        ],
        out_specs=pl.BlockSpec((DFT_TM, d), lambda b, m, k: (b * mt + m, 0)),
        out_shape=jax.ShapeDtypeStruct(x2d.shape, F32),
        scratch_shapes=[pltpu.VMEM((DFT_TM, d), F32)],
        compiler_params=_params(3),
        name="fnet_seq",
    )(ac, asn, p, q, wo, bo, x2d)


def _dft_tables(seq):
    n = jnp.arange(seq, dtype=I32)
    kn = (n[:, None] * n[None, :]) % seq
    ang = kn.astype(F32) * F32(2.0 * math.pi / seq)
    s_seq = F32(1.0 / math.sqrt(seq))
    ac = (jnp.cos(ang) * s_seq).astype(BF16)
    asn = (-jnp.sin(ang) * s_seq).astype(BF16)
    c = jnp.arange(FNET_GROUP_DIM, dtype=I32)
    cc = (c[:, None] * c[None, :]) % FNET_GROUP_DIM
    angc = cc.astype(F32) * F32(2.0 * math.pi / FNET_GROUP_DIM)
    s_ch = F32(1.0 / math.sqrt(FNET_GROUP_DIM))
    cs = jnp.concatenate([jnp.cos(angc) * s_ch, jnp.sin(angc) * s_ch], axis=1).astype(BF16)
    return ac, asn, cs


def _fourier_layer(x2d, g, w_o, b_o, batch, seq):
    ac, asn, cs = _dft_tables(seq)
    p, q = _fnet_chan(x2d, g.reshape(1, -1), cs)
    return _fnet_seq(ac, asn, p, q, w_o.astype(BF16), b_o.reshape(1, -1), x2d, batch, seq)


def _dup_halves(x, lo_half):
    swapped = pltpu.roll(x, HEAD_DIM, axis=1)
    return jnp.where(lo_half, x, swapped), jnp.where(lo_half, swapped, x)


def _qkv_kernel(x_ref, g_ref, w_ref, b_ref, cos_ref, sin_ref, q_ref, k_ref, v_ref):
    h = _rms(x_ref[...], g_ref[...]).astype(BF16)
    qkv = jnp.dot(h, w_ref[...], preferred_element_type=F32) + b_ref[...]
    cos = cos_ref[...]
    sin = sin_ref[...]
    lane = lax.broadcasted_iota(I32, cos.shape, 1)
    first_half = (lane % HEAD_DIM) < (HEAD_DIM // 2)
    lo_half = lane < HEAD_DIM
    n_q = N_HEADS * HEAD_DIM // LANES
    n_kv = KV_DIM // LANES

    def rope(xc):
        fwd = pltpu.roll(xc, LANES - HEAD_DIM // 2, axis=1)
        bwd = pltpu.roll(xc, HEAD_DIM // 2, axis=1)
        return xc * cos + jnp.where(first_half, fwd, bwd) * sin

    for c in range(n_q):
        r = rope(qkv[:, c * LANES:(c + 1) * LANES])
        q_ref[:, c * LANES:(c + 1) * LANES] = (r * F32(HEAD_DIM ** -0.5)).astype(BF16)
    for c in range(n_kv):
        kc = rope(qkv[:, (n_q + c) * LANES:(n_q + c + 1) * LANES])
        vc = qkv[:, (n_q + n_kv + c) * LANES:(n_q + n_kv + c + 1) * LANES]
        for ref, val in ((k_ref, kc), (v_ref, vc)):
            a, b = _dup_halves(val, lo_half)
            ref[:, (2 * c) * LANES:(2 * c + 1) * LANES] = a.astype(BF16)
            ref[:, (2 * c + 1) * LANES:(2 * c + 2) * LANES] = b.astype(BF16)


def _qkv_rope(x2d, g, w_qkv, b_qkv, seq):
    t, d = x2d.shape
    qd = N_HEADS * HEAD_DIM
    kvd = N_KV_HEADS * LANES
    pos = jnp.arange(seq, dtype=F32)
    inv_freq = ROPE_THETA ** (-jnp.arange(0, HEAD_DIM, 2, dtype=F32) / HEAD_DIM)
    ang = pos[:, None] * inv_freq[None, :]
    cos = jnp.cos(ang)
    sin = jnp.sin(ang)
    reps = LANES // (HEAD_DIM // 2)
    cos_l = jnp.tile(cos, (1, reps))
    sin_l = jnp.tile(jnp.concatenate([-sin, sin], axis=1), (1, LANES // HEAD_DIM))
    spt = seq // ROW_TILE
    return pl.pallas_call(
        _qkv_kernel,
        grid=(t // ROW_TILE,),
        in_specs=[
            pl.BlockSpec((ROW_TILE, d), lambda i: (i, 0)),
            pl.BlockSpec((1, d), lambda i: (0, 0)),
            pl.BlockSpec(w_qkv.shape, lambda i: (0, 0)),
            pl.BlockSpec((1, w_qkv.shape[1]), lambda i: (0, 0)),
            pl.BlockSpec((ROW_TILE, LANES), lambda i: (i % spt, 0)),
            pl.BlockSpec((ROW_TILE, LANES), lambda i: (i % spt, 0)),
        ],
        out_specs=[
            pl.BlockSpec((ROW_TILE, qd), lambda i: (i, 0)),
            pl.BlockSpec((ROW_TILE, kvd), lambda i: (i, 0)),
            pl.BlockSpec((ROW_TILE, kvd), lambda i: (i, 0)),
        ],
        out_shape=[
            jax.ShapeDtypeStruct((t, qd), BF16),
            jax.ShapeDtypeStruct((t, kvd), BF16),
            jax.ShapeDtypeStruct((t, kvd), BF16),
        ],
        compiler_params=_params(1),
        name="qkv_rope",
    )(x2d, g.reshape(1, -1), w_qkv.astype(BF16), b_qkv.reshape(1, -1), cos_l, sin_l)


def _attn_kernel(sink_ref, q_ref, kp_ref, kc_ref, kn_ref, vp_ref, vc_ref, vn_ref, o_ref, *, n_blocks):
    j = pl.program_id(1)
    blk = ATTN_BLOCK
    kcat = jnp.concatenate([kp_ref[...], kc_ref[...], kn_ref[...]], axis=0)
    vcat = jnp.concatenate([vp_ref[...], vc_ref[...], vn_ref[...]], axis=0)
    iq = lax.broadcasted_iota(I32, (blk, 3 * blk), 0)
    ik = lax.broadcasted_iota(I32, (blk, 3 * blk), 1)
    kpos = (j - 1) * blk + ik
    mask = (jnp.abs(iq + blk - ik) <= WINDOW) & (kpos >= 0) & (kpos < n_blocks * blk)
    mask = jnp.concatenate([mask] * Q_PER_KV, axis=0)
    lo_half = lax.broadcasted_iota(I32, (1, LANES), 1) < HEAD_DIM
    ones = jnp.ones((3 * blk, LANES), BF16)
    zero = jnp.zeros((), BF16)
    half = Q_PER_KV // 2 * blk
    for g in range(N_KV_HEADS):
        kd = kcat[:, g * LANES:(g + 1) * LANES]
        vd = vcat[:, g * LANES:(g + 1) * LANES]
        c0 = g * Q_PER_KV // 2
        qa = q_ref[:, c0 * LANES:(c0 + 1) * LANES]
        qb = q_ref[:, (c0 + 1) * LANES:(c0 + 2) * LANES]
        order = (0, 2, 1, 3)
        qs = jnp.concatenate([jnp.where(lo_half, qa, zero), jnp.where(lo_half, qb, zero),
                              jnp.where(lo_half, zero, qa), jnp.where(lo_half, zero, qb)], axis=0)
        s = _nt_dot(qs, kd)
        s = jnp.where(mask, s, NEG_INF)
        sink = jnp.concatenate([jnp.full((blk, 1), sink_ref[g * Q_PER_KV + u], F32) for u in order], axis=0)
        mx = jnp.maximum(jnp.max(s, axis=-1, keepdims=True), sink)
        p = jnp.exp(s - mx).astype(BF16)
        esink = jnp.exp(sink - mx)
        v_lo = jnp.concatenate([jnp.where(lo_half, vd, zero), ones], axis=1)
        v_hi = jnp.concatenate([jnp.where(lo_half, zero, vd), ones], axis=1)
        r_lo = jnp.dot(p[:half], v_lo, preferred_element_type=F32)
        r_hi = jnp.dot(p[half:], v_hi, preferred_element_type=F32)
        n_lo = r_lo[:, :LANES] / (r_lo[:, LANES:] + esink[:half])
        n_hi = r_hi[:, :LANES] / (r_hi[:, LANES:] + esink[half:])
        o_ref[:, c0 * LANES:(c0 + 1) * LANES] = (n_lo[:blk] + n_hi[:blk]).astype(BF16)
        o_ref[:, (c0 + 1) * LANES:(c0 + 2) * LANES] = (n_lo[blk:] + n_hi[blk:]).astype(BF16)


def _attention(q, k, v, sink, batch, seq):
    blk = ATTN_BLOCK
    nb = seq // blk
    qd = q.shape[1]
    kvd = k.shape[1]

    def kv_spec(off):
        return pl.BlockSpec((blk, kvd), lambda b, j: (b * nb + jnp.clip(j + off, 0, nb - 1), 0))

    return pl.pallas_call(
        functools.partial(_attn_kernel, n_blocks=nb),
        grid=(batch, nb),
        in_specs=[
            pl.BlockSpec(memory_space=pltpu.SMEM),
            pl.BlockSpec((blk, qd), lambda b, j: (b * nb + j, 0)),
            kv_spec(-1), kv_spec(0), kv_spec(1),
            kv_spec(-1), kv_spec(0), kv_spec(1),
        ],
        out_specs=pl.BlockSpec((blk, qd), lambda b, j: (b * nb + j, 0)),
        out_shape=jax.ShapeDtypeStruct(q.shape, BF16),
        compiler_params=_params(2),
        name="band_attn",
    )(sink.astype(F32), q, k, k, k, v, v, v)


def _proj_res_kernel(a_ref, w_ref, b_ref, x_ref, o_ref):
    o_ref[...] = (jnp.dot(a_ref[...], w_ref[...], preferred_element_type=F32)
                  + b_ref[...] + x_ref[...])


def _proj_residual(a, w, b, x2d):
    t, d = x2d.shape
    return pl.pallas_call(
        _proj_res_kernel,
        grid=(t // ROW_TILE,),
        in_specs=[
            pl.BlockSpec((ROW_TILE, a.shape[1]), lambda i: (i, 0)),
            pl.BlockSpec(w.shape, lambda i: (0, 0)),
            pl.BlockSpec((1, d), lambda i: (0, 0)),
            pl.BlockSpec((ROW_TILE, d), lambda i: (i, 0)),
        ],
        out_specs=pl.BlockSpec((ROW_TILE, d), lambda i: (i, 0)),
        out_shape=jax.ShapeDtypeStruct(x2d.shape, F32),
        compiler_params=_params(1),
        name="proj_residual",
    )(a, w.astype(BF16), b.reshape(1, -1), x2d)


def _attention_layer(x2d, g, w_qkv, b_qkv, sink, w_o, b_o, batch, seq):
    q, k, v = _qkv_rope(x2d, g, w_qkv, b_qkv, seq)
    o = _attention(q, k, v, sink, batch, seq)
    return _proj_residual(o, w_o, b_o, x2d)


def _router_kernel(x_ref, g_ref, whi_ref, wlo_ref, rb_ref, ut_ref,
                   h_ref, e_ref, gate_ref, rank_ref, cnt_ref, carry_ref):
    i = pl.program_id(0)

    @pl.when(i == 0)
    def _():
        carry_ref[...] = jnp.zeros_like(carry_ref)

    h = _rms(x_ref[...], g_ref[...])
    for ch in range(ROW_CHUNKS):
        h_ref[pl.ds(ch, h.shape[0], stride=ROW_CHUNKS), :] = h[:, ch * LANES:(ch + 1) * LANES]
    h_hi = h.astype(BF16)
    h_lo = (h - h_hi.astype(F32)).astype(BF16)
    logits = (_nt_dot(whi_ref[...], h_hi) + _nt_dot(whi_ref[...], h_lo)
              + _nt_dot(wlo_ref[...], h_hi) + rb_ref[...])
    n_e, tm = logits.shape
    eid = lax.broadcasted_iota(I32, (n_e, tm), 0)
    vals = logits
    top_v, top_i, hots = [], [], []
    for _ in range(TOP_K):
        mx = jnp.max(vals, axis=0, keepdims=True)
        idx = jnp.min(jnp.where(vals == mx, eid, n_e), axis=0, keepdims=True)
        hot = eid == idx
        vals = jnp.where(hot, -jnp.inf, vals)
        top_v.append(mx)
        top_i.append(idx)
        hots.append(hot)
    ex = [jnp.exp(v - top_v[0]) for v in top_v]
    den = ex[0] + ex[1] + ex[2] + ex[3]
    member = (hots[0] | hots[1] | hots[2] | hots[3]).astype(F32)
    cum = jnp.dot(member.astype(BF16), ut_ref[...], preferred_element_type=F32)
    carry = carry_ref[:, 0:1]
    excl = carry + cum - member
    for kk in range(TOP_K):
        e_ref[kk:kk + 1, :] = top_i[kk]
        gate_ref[kk:kk + 1, :] = ex[kk] / den
        rank_ref[kk:kk + 1, :] = jnp.sum(jnp.where(hots[kk], excl, 0.0), axis=0, keepdims=True).astype(I32)
    new_carry = carry + jnp.sum(member, axis=1, keepdims=True)
    carry_ref[...] = jnp.broadcast_to(new_carry, carry_ref.shape)
    cnt_ref[...] = jnp.broadcast_to(new_carry, cnt_ref.shape)


def _router(x2d, g, router_w, router_b):
    t, d = x2d.shape
    tm = ROW_TILE
    wt = router_w.T.astype(F32)
    w_hi = wt.astype(BF16)
    w_lo = (wt - w_hi.astype(F32)).astype(BF16)
    ut = (jnp.arange(tm, dtype=I32)[:, None] <= jnp.arange(tm, dtype=I32)[None, :]).astype(BF16)
    kt_spec = pl.BlockSpec((TOP_K, tm), lambda i: (0, i))
    return pl.pallas_call(
        _router_kernel,
        grid=(t // tm,),
        in_specs=[
            pl.BlockSpec((tm, d), lambda i: (i, 0)),
            pl.BlockSpec((1, d), lambda i: (0, 0)),
            pl.BlockSpec((N_EXPERTS, d), lambda i: (0, 0)),
            pl.BlockSpec((N_EXPERTS, d), lambda i: (0, 0)),
            pl.BlockSpec((N_EXPERTS, 1), lambda i: (0, 0)),
            pl.BlockSpec((tm, tm), lambda i: (0, 0)),
        ],
        out_specs=[
            pl.BlockSpec((tm * ROW_CHUNKS, LANES), lambda i: (i, 0)),
            kt_spec, kt_spec, kt_spec,
            pl.BlockSpec((N_EXPERTS, LANES), lambda i: (0, 0)),
        ],
        out_shape=[
            jax.ShapeDtypeStruct((t * ROW_CHUNKS, LANES), F32),
            jax.ShapeDtypeStruct((TOP_K, t), I32),
            jax.ShapeDtypeStruct((TOP_K, t), F32),
            jax.ShapeDtypeStruct((TOP_K, t), I32),
            jax.ShapeDtypeStruct((N_EXPERTS, LANES), F32),
        ],
        scratch_shapes=[pltpu.VMEM((N_EXPERTS, LANES), F32)],
        compiler_params=_params(1),
        name="router",
    )(x2d, g.reshape(1, -1), w_hi, w_lo, router_b.reshape(-1, 1).astype(F32), ut)


def _expert_kernel(be_ref, base_ref, idx_hbm, h_hbm, wgu_ref, bgu_ref, wd_ref, bd_ref, y_ref,
                   idx0, idx1, xbuf0, xbuf1, wgu_bf, wd_bf, isem, gsem, *, n_blocks):
    i = pl.program_id(0)
    last_blk = n_blocks - 1
    bm = y_ref.shape[0] // ROW_CHUNKS
    d_ff = wd_bf.shape[0]
    idx_smem = (idx0, idx1)
    xbuf = (xbuf0, xbuf1)
    win = idx0.shape[0]
    nch = ROW_CHUNKS
    c = jnp.maximum(i - 1, 0)

    def idx_copy(blk, s):
        start = pl.multiple_of((base_ref[blk] // LANES) * LANES, LANES)
        return pltpu.make_async_copy(idx_hbm.at[pl.ds(start, win)], idx_smem[s], isem.at[s])

    def rows_done(s):
        return pltpu.make_async_copy(h_hbm.at[pl.ds(0, bm * nch), :], xbuf[s], gsem.at[s])

    @pl.when(i == 0)
    def _():
        idx_copy(0, 0).start()
        rows_done(1).start()

    @pl.when((i <= 1) | (be_ref[c] != be_ref[jnp.maximum(c - 1, 0)]))
    def _():
        wgu_bf[...] = wgu_ref[0, 0].astype(BF16)
        wd_bf[...] = wd_ref[0, 0].astype(BF16)

    def step(s):
        blk = jnp.minimum(i, last_blk)
        idx_copy(blk, s).wait()
        rows_done(1 - s).wait()
        idx_copy(jnp.minimum(i + 1, last_blk), 1 - s).start()
        off = lax.rem(base_ref[blk], LANES)
        for r in range(bm):
            tok = idx_smem[s][off + r]
            src = h_hbm.at[pl.ds(pl.multiple_of(tok * nch, nch), nch), :]
            pltpu.make_async_copy(src, xbuf[s].at[pl.ds(r * nch, nch), :], gsem.at[s]).start()
        x = jnp.concatenate([xbuf[1 - s][pl.ds(ch, bm, stride=nch), :] for ch in range(nch)], axis=1)
        gu = jnp.dot(x.astype(BF16), wgu_bf[...], preferred_element_type=F32) + bgu_ref[0, 0]
        gate = jnp.minimum(gu[:, :d_ff], SWIGLU_LIMIT)
        up = jnp.clip(gu[:, d_ff:], -SWIGLU_LIMIT, SWIGLU_LIMIT)
        glu = gate * jax.nn.sigmoid(gate * SWIGLU_ALPHA)
        act = ((up + 1.0) * glu).astype(BF16)
        y = jnp.dot(act, wd_bf[...], preferred_element_type=F32) + bd_ref[0, 0]
        for ch in range(nch):
            y_ref[pl.ds(ch, bm, stride=nch), :] = y[:, ch * LANES:(ch + 1) * LANES]

    for s in range(2):
        pl.when(lax.rem(i, 2) == s)(functools.partial(step, s))

    @pl.when(i == n_blocks)
    def _():
        s_last = n_blocks % 2
        idx_copy(last_blk, 1 - s_last).wait()
        rows_done(s_last).wait()


def _experts(h8, sorted_tok, block_expert, block_base, layer, w_gu, b_gu, w_down, b_down, n_blocks):
    bm = MOE_BM
    depth, n_e, d_ff, d = w_down.shape

    def cidx(i):
        return jnp.maximum(i - 1, 0)

    def wmap(i, be, base):
        return (layer, be[cidx(i)], 0, 0)

    grid_spec = pltpu.PrefetchScalarGridSpec(
        num_scalar_prefetch=2,
        grid=(n_blocks + 1,),
        in_specs=[
            pl.BlockSpec(memory_space=pl.ANY),
            pl.BlockSpec(memory_space=pl.ANY),
            pl.BlockSpec((1, 1, d, 2 * d_ff), wmap),
            pl.BlockSpec((1, 1, 1, 2 * d_ff), wmap),
            pl.BlockSpec((1, 1, d_ff, d), wmap),
            pl.BlockSpec((1, 1, 1, d), wmap),
        ],
        out_specs=pl.BlockSpec((bm * ROW_CHUNKS, LANES), lambda i, be, base: (cidx(i), 0)),
        scratch_shapes=[
            pltpu.SMEM((bm + LANES,), I32),
            pltpu.SMEM((bm + LANES,), I32),
            pltpu.VMEM((bm * ROW_CHUNKS, LANES), F32),
            pltpu.VMEM((bm * ROW_CHUNKS, LANES), F32),
            pltpu.VMEM((d, 2 * d_ff), BF16),
            pltpu.VMEM((d_ff, d), BF16),
            pltpu.SemaphoreType.DMA((2,)),
            pltpu.SemaphoreType.DMA((2,)),
        ],
    )
    return pl.pallas_call(
        functools.partial(_expert_kernel, n_blocks=n_blocks),
        grid_spec=grid_spec,
        out_shape=jax.ShapeDtypeStruct((n_blocks * bm * ROW_CHUNKS, LANES), F32),
        compiler_params=_params(1),
        name="moe_experts",
    )(block_expert, block_base, sorted_tok, h8, w_gu, b_gu.reshape(depth, n_e, 1, -1), w_down,
      b_down.reshape(depth, n_e, 1, -1))


def _combine_kernel(didx_hbm, ys_hbm, x_ref, gt_ref, fg_ref, o_ref, idx0, idx1, buf0, buf1, isem, gsem,
                    *, n_tiles, final_norm):
    i = pl.program_id(0)
    last = n_tiles - 1
    idx_smem = (idx0, idx1)
    buf = (buf0, buf1)
    rows = idx0.shape[0]
    tc = rows // TOP_K
    nch = ROW_CHUNKS

    def idx_copy(blk, s):
        return pltpu.make_async_copy(didx_hbm.at[pl.ds(blk * rows, rows)], idx_smem[s], isem.at[s])

    def rows_done(s):
        return pltpu.make_async_copy(ys_hbm.at[pl.ds(0, rows * nch), :], buf[s], gsem.at[s])

    @pl.when(i == 0)
    def _():
        idx_copy(0, 0).start()
        rows_done(1).start()

    def step(s):
        idx_copy(jnp.minimum(i, last), s).wait()
        rows_done(1 - s).wait()
        idx_copy(jnp.minimum(i + 1, last), 1 - s).start()
        for r in range(rows):
            src = ys_hbm.at[pl.ds(pl.multiple_of(idx_smem[s][r] * nch, nch), nch), :]
            pltpu.make_async_copy(src, buf[s].at[pl.ds(r * nch, nch), :], gsem.at[s]).start()
        acc = x_ref[...]
        gt = gt_ref[...]
        for kk in range(TOP_K):
            yk = jnp.concatenate([buf[1 - s][pl.ds(kk * tc * nch + ch, tc, stride=nch), :]
                                  for ch in range(nch)], axis=1)
            acc = acc + gt[:, kk:kk + 1] * yk
        if final_norm:
            acc = _rms(acc, fg_ref[...])
        o_ref[...] = acc

    for s in range(2):
        pl.when(lax.rem(i, 2) == s)(functools.partial(step, s))

    @pl.when(i == n_tiles)
    def _():
        s_last = n_tiles % 2
        idx_copy(last, 1 - s_last).wait()
        rows_done(s_last).wait()


def _combine(ys8, dest_tiles, gates_t, x2d, final_g):
    t, d = x2d.shape
    tc = COMBINE_TILE
    n = t // tc
    rows = TOP_K * tc

    def cidx(i):
        return jnp.maximum(i - 1, 0)

    fg = jnp.ones((1, d), F32) if final_g is None else final_g.reshape(1, -1).astype(F32)
    return pl.pallas_call(
        functools.partial(_combine_kernel, n_tiles=n, final_norm=final_g is not None),
        grid=(n + 1,),
        in_specs=[
            pl.BlockSpec(memory_space=pl.ANY),
            pl.BlockSpec(memory_space=pl.ANY),
            pl.BlockSpec((tc, d), lambda i: (cidx(i), 0)),
            pl.BlockSpec((tc, TOP_K), lambda i: (cidx(i), 0)),
            pl.BlockSpec((1, d), lambda i: (0, 0)),
        ],
        out_specs=pl.BlockSpec((tc, d), lambda i: (cidx(i), 0)),
        out_shape=jax.ShapeDtypeStruct((t, d), F32),
        scratch_shapes=[
            pltpu.SMEM((rows,), I32),
            pltpu.SMEM((rows,), I32),
            pltpu.VMEM((rows * ROW_CHUNKS, LANES), F32),
            pltpu.VMEM((rows * ROW_CHUNKS, LANES), F32),
            pltpu.SemaphoreType.DMA((2,)),
            pltpu.SemaphoreType.DMA((2,)),
        ],
        compiler_params=_params(1),
        name="moe_combine",
    )(dest_tiles, ys8, x2d, gates_t, fg)


def _moe_layer(x2d, g, router_w, router_b, layer, w_gu, b_gu, w_down, b_down, final_g):
    t, d = x2d.shape
    bm = MOE_BM
    a = t * TOP_K
    n_blocks = a // bm + N_EXPERTS

    h, e_kt, gate_kt, rank_kt, cnt = _router(x2d, g, router_w, router_b)

    counts = cnt[:, 0].astype(I32)
    blocks_e = (counts + bm - 1) // bm
    blk_end = jnp.cumsum(blocks_e)
    blk_start = blk_end - blocks_e
    n_valid = blk_end[-1]
    tok_start = jnp.cumsum(counts) - counts

    a_id = jnp.arange(t, dtype=I32)[None, :] * TOP_K + jnp.arange(TOP_K, dtype=I32)[:, None]
    sorted_key = jnp.sort((e_kt * a + a_id).reshape(-1))
    sorted_tok = jnp.pad((sorted_key % a) // TOP_K, (0, bm + LANES))

    eids = jnp.arange(N_EXPERTS, dtype=I32)
    bid = jnp.arange(n_blocks, dtype=I32)
    be = jnp.minimum(jnp.sum((bid[:, None] >= blk_end[None, :]).astype(I32), axis=1), N_EXPERTS - 1)
    hot = be[:, None] == eids[None, :]
    base = (jnp.sum(jnp.where(hot, tok_start[None, :], 0), axis=1)
            + (bid - jnp.sum(jnp.where(hot, blk_start[None, :], 0), axis=1)) * bm)
    valid = bid < n_valid
    last_e = jnp.sum(jnp.where(bid == n_valid - 1, be, 0))
    block_expert = jnp.where(valid, be, last_e).astype(I32)
    block_base = jnp.where(valid, base, 0).astype(I32)

    ys = _experts(h, sorted_tok, block_expert, block_base, layer, w_gu, b_gu, w_down, b_down, n_blocks)

    dest = rank_kt
    for e in range(N_EXPERTS):
        dest = dest + jnp.where(e_kt == e, blk_start[e] * bm, 0)
    tc = COMBINE_TILE
    dest_tiles = dest.reshape(TOP_K, t // tc, tc).transpose(1, 0, 2).reshape(-1).astype(I32)
    return _combine(ys, dest_tiles, gate_kt.T, x2d, final_g)


def kernel(x, norm_mix_g, norm_ffn_g, fnet_w_o, fnet_b_o, attn_w_qkv, attn_b_qkv, attn_sink, attn_w_o, attn_b_o, router_w, router_b, expert_w_gu, expert_b_gu, expert_w_down, expert_b_down, final_norm_g):
    batch, seq, d = x.shape
    depth = norm_mix_g.shape[0]
    x2d = x.reshape(batch * seq, d)
    for i in range(depth):
        j = i // 2
        if i % 2 == 0:
            x2d = _fourier_layer(x2d, norm_mix_g[i], fnet_w_o[j], fnet_b_o[j], batch, seq)
        else:
            x2d = _attention_layer(x2d, norm_mix_g[i], attn_w_qkv[j], attn_b_qkv[j], attn_sink[j],
                                   attn_w_o[j], attn_b_o[j], batch, seq)
        fg = final_norm_g if i == depth - 1 else None
        x2d = _moe_layer(x2d, norm_ffn_g[i], router_w[i], router_b[i], i, expert_w_gu, expert_b_gu,
                         expert_w_down, expert_b_down, fg)
    return x2d.reshape(batch, seq, d)
```

```python
import functools
import math

import jax
import jax.numpy as jnp
from jax import lax
from jax.experimental import pallas as pl
from jax.experimental.pallas import tpu as pltpu

F32 = jnp.float32
BF16 = jnp.bfloat16
I32 = jnp.int32

D_MODEL = 1024
FNET_GROUPS = 8
FNET_GROUP_DIM = D_MODEL // FNET_GROUPS
HEAD_DIM = 64
N_HEADS = 16
N_KV_HEADS = 4
Q_PER_KV = N_HEADS // N_KV_HEADS
KV_DIM = N_KV_HEADS * HEAD_DIM
WINDOW = 128
ATTN_BLOCK = 128
ROPE_THETA = 10000.0
N_EXPERTS = 32
TOP_K = 4
D_FF = D_MODEL
SWIGLU_ALPHA = 1.702
SWIGLU_LIMIT = 7.0
RMS_EPS = 1e-5
NEG_INF = -1e30

LANES = 128
ROW_CHUNKS = D_MODEL // LANES
VMEM_LIMIT = 56 * 1024 * 1024

ROW_TILE = 512
MOE_BM = 256
COMBINE_TILE = 128
ATTN_Q_BLOCKS = 2
DFT_CHAN_TILE = 256
DFT_TQ = 256
DFT_TK = 512

_ARB = pltpu.ARBITRARY


def _params(n_axes):
    return pltpu.CompilerParams(dimension_semantics=(_ARB,) * n_axes, vmem_limit_bytes=VMEM_LIMIT)


def _nt_dot(a, b):
    return lax.dot_general(a, b, (((1,), (1,)), ((), ())), preferred_element_type=F32)


def _rms(x, g):
    ms = jnp.mean(x * x, axis=-1, keepdims=True)
    return (x * lax.rsqrt(ms + RMS_EPS)) * g


DFT_RADIX = 4


def _fnet_chan_kernel(x0_ref, x1_ref, x2_ref, x3_ref, g_ref, cs_ref, a_ref, b_ref):
    hs = [_rms(xr[...], g_ref[...]).astype(BF16) for xr in (x0_ref, x1_ref, x2_ref, x3_ref)]
    gd = FNET_GROUP_DIM
    for gi in range(FNET_GROUPS):
        lo = gi * gd
        pq = [jnp.dot(h[:, lo:lo + gd], cs_ref[...], preferred_element_type=F32) for h in hs]
        p = [v[:, :gd] for v in pq]
        q = [v[:, gd:] for v in pq]
        dp02, dp13 = p[0] - p[2], p[1] - p[3]
        dq02, dq13 = q[0] - q[2], q[1] - q[3]
        sp02, sp13 = p[0] + p[2], p[1] + p[3]
        sq02, sq13 = q[0] + q[2], q[1] + q[3]
        a = (sp02 + sp13, dp02 - dq13, sp02 - sp13, dp02 + dq13)
        b = (-(sq02 + sq13), -dq02 - dp13, -(sq02 - sq13), dp13 - dq02)
        for r in range(DFT_RADIX):
            a_ref[r, :, lo:lo + gd] = a[r].astype(BF16)
            b_ref[r, :, lo:lo + gd] = b[r].astype(BF16)


def _fnet_chan(x2d, g, cs, batch, seq):
    t, d = x2d.shape
    qlen = seq // DFT_RADIX
    tm = DFT_CHAN_TILE
    per_q = qlen // tm

    def x_spec(j):
        return pl.BlockSpec((tm, d), lambda b, m: (b * (seq // tm) + j * per_q + m, 0))

    out_spec = pl.BlockSpec((DFT_RADIX, tm, d), lambda b, m: (0, b * per_q + m, 0))
    out_sds = jax.ShapeDtypeStruct((DFT_RADIX, batch * qlen, d), BF16)
    return pl.pallas_call(
        _fnet_chan_kernel,
        grid=(batch, per_q),
        in_specs=[x_spec(0), x_spec(1), x_spec(2), x_spec(3),
                  pl.BlockSpec((1, d), lambda b, m: (0, 0)),
                  pl.BlockSpec(cs.shape, lambda b, m: (0, 0))],
        out_specs=[out_spec, out_spec],
        out_shape=[out_sds, out_sds],
        compiler_params=_params(2),
        name="fnet_chan",
    )(x2d, x2d, x2d, x2d, g, cs)


def _fnet_seq_kernel(c_ref, s_ref, a_ref, b_ref, e_ref, wo_ref, bo_ref, x_ref, o_ref, acc_ref):
    k = pl.program_id(2)

    @pl.when(k == 0)
    def _():
        acc_ref[...] = jnp.zeros_like(acc_ref)

    for r in range(DFT_RADIX):
        acc_ref[r] += (jnp.dot(c_ref[r], a_ref[r], preferred_element_type=F32)
                       + jnp.dot(s_ref[r], b_ref[r], preferred_element_type=F32))

    @pl.when(k == pl.num_programs(2) - 1)
    def _():
        mixed = jnp.dot(e_ref[0], acc_ref[0].astype(BF16), preferred_element_type=F32)
        for r in range(1, DFT_RADIX):
            mixed += jnp.dot(e_ref[r], acc_ref[r].astype(BF16), preferred_element_type=F32)
        o_ref[...] = (jnp.dot(mixed.astype(BF16), wo_ref[...], preferred_element_type=F32)
                      + bo_ref[...] + x_ref[...])


def _fnet_seq(ctab, stab, a, b, wo, bo, x2d, batch, seq):
    d = x2d.shape[1]
    qlen = seq // DFT_RADIX
    tq, tk = DFT_TQ, DFT_TK
    qt, kt = qlen // tq, qlen // tk
    tab_spec = pl.BlockSpec((DFT_RADIX, tq, tk), lambda bb, i, k: (0, i, k))
    in_spec = pl.BlockSpec((DFT_RADIX, tk, d), lambda bb, i, k: (0, bb * kt + k, 0))
    row_spec = pl.BlockSpec((DFT_RADIX * tq, d), lambda bb, i, k: (bb * qt + i, 0))
    out_row = jnp.arange(DFT_RADIX * tq, dtype=I32)[None, :, None]
    src_row = jnp.arange(tq, dtype=I32)[None, None, :]
    res = jnp.arange(DFT_RADIX, dtype=I32)[:, None, None]
    expand = (out_row == DFT_RADIX * src_row + res).astype(BF16)
    return pl.pallas_call(
        _fnet_seq_kernel,
        grid=(batch, qt, kt),
        in_specs=[tab_spec, tab_spec, in_spec, in_spec,
                  pl.BlockSpec(expand.shape, lambda bb, i, k: (0, 0, 0)),
                  pl.BlockSpec((d, d), lambda bb, i, k: (0, 0)),
                  pl.BlockSpec((1, d), lambda bb, i, k: (0, 0)),
                  row_spec],
        out_specs=row_spec,
        out_shape=jax.ShapeDtypeStruct(x2d.shape, F32),
        scratch_shapes=[pltpu.VMEM((DFT_RADIX, tq, d), F32)],
        compiler_params=_params(3),
        name="fnet_seq",
    )(ctab, stab, a, b, expand, wo, bo, x2d)


def _dft_tables(seq):
    qlen = seq // DFT_RADIX
    m = jnp.arange(qlen, dtype=I32)
    freq = DFT_RADIX * m[None, :, None] + jnp.arange(DFT_RADIX, dtype=I32)[:, None, None]
    ang = ((freq * m[None, None, :]) % seq).astype(F32) * F32(2.0 * math.pi / seq)
    s_seq = F32(1.0 / math.sqrt(seq))
    ctab = (jnp.cos(ang) * s_seq).astype(BF16)
    stab = (jnp.sin(ang) * s_seq).astype(BF16)
    c = jnp.arange(FNET_GROUP_DIM, dtype=I32)
    cc = (c[:, None] * c[None, :]) % FNET_GROUP_DIM
    angc = cc.astype(F32) * F32(2.0 * math.pi / FNET_GROUP_DIM)
    s_ch = F32(1.0 / math.sqrt(FNET_GROUP_DIM))
    cs = jnp.concatenate([jnp.cos(angc) * s_ch, jnp.sin(angc) * s_ch], axis=1).astype(BF16)
    return ctab, stab, cs


def _fourier_layer(x2d, g, w_o, b_o, batch, seq):
    ctab, stab, cs = _dft_tables(seq)
    a, b = _fnet_chan(x2d, g.reshape(1, -1), cs, batch, seq)
    return _fnet_seq(ctab, stab, a, b, w_o.astype(BF16), b_o.reshape(1, -1), x2d, batch, seq)


def _dup_halves(x, lo_half):
    swapped = pltpu.roll(x, HEAD_DIM, axis=1)
    return jnp.where(lo_half, x, swapped), jnp.where(lo_half, swapped, x)


def _qkv_kernel(x_ref, g_ref, w_ref, b_ref, cos_ref, sin_ref, q_ref, k_ref, v_ref):
    h = _rms(x_ref[...], g_ref[...]).astype(BF16)
    qkv = jnp.dot(h, w_ref[...], preferred_element_type=F32) + b_ref[...]
    cos = cos_ref[...]
    sin = sin_ref[...]
    lane = lax.broadcasted_iota(I32, cos.shape, 1)
    first_half = (lane % HEAD_DIM) < (HEAD_DIM // 2)
    lo_half = lane < HEAD_DIM
    n_q = N_HEADS * HEAD_DIM // LANES
    n_kv = KV_DIM // LANES

    def rope(xc):
        fwd = pltpu.roll(xc, LANES - HEAD_DIM // 2, axis=1)
        bwd = pltpu.roll(xc, HEAD_DIM // 2, axis=1)
        return xc * cos + jnp.where(first_half, fwd, bwd) * sin

    for c in range(n_q):
        r = rope(qkv[:, c * LANES:(c + 1) * LANES])
        q_ref[:, c * LANES:(c + 1) * LANES] = (r * F32(HEAD_DIM ** -0.5)).astype(BF16)
    for c in range(n_kv):
        kc = rope(qkv[:, (n_q + c) * LANES:(n_q + c + 1) * LANES])
        vc = qkv[:, (n_q + n_kv + c) * LANES:(n_q + n_kv + c + 1) * LANES]
        for ref, val in ((k_ref, kc), (v_ref, vc)):
            a, b = _dup_halves(val, lo_half)
            ref[:, (2 * c) * LANES:(2 * c + 1) * LANES] = a.astype(BF16)
            ref[:, (2 * c + 1) * LANES:(2 * c + 2) * LANES] = b.astype(BF16)


def _qkv_rope(x2d, g, w_qkv, b_qkv, seq):
    t, d = x2d.shape
    qd = N_HEADS * HEAD_DIM
    kvd = N_KV_HEADS * LANES
    pos = jnp.arange(seq, dtype=F32)
    inv_freq = ROPE_THETA ** (-jnp.arange(0, HEAD_DIM, 2, dtype=F32) / HEAD_DIM)
    ang = pos[:, None] * inv_freq[None, :]
    cos = jnp.cos(ang)
    sin = jnp.sin(ang)
    reps = LANES // (HEAD_DIM // 2)
    cos_l = jnp.tile(cos, (1, reps))
    sin_l = jnp.tile(jnp.concatenate([-sin, sin], axis=1), (1, LANES // HEAD_DIM))
    spt = seq // ROW_TILE
    return pl.pallas_call(
        _qkv_kernel,
        grid=(t // ROW_TILE,),
        in_specs=[
            pl.BlockSpec((ROW_TILE, d), lambda i: (i, 0)),
            pl.BlockSpec((1, d), lambda i: (0, 0)),
            pl.BlockSpec(w_qkv.shape, lambda i: (0, 0)),
            pl.BlockSpec((1, w_qkv.shape[1]), lambda i: (0, 0)),
            pl.BlockSpec((ROW_TILE, LANES), lambda i: (i % spt, 0)),
            pl.BlockSpec((ROW_TILE, LANES), lambda i: (i % spt, 0)),
        ],
        out_specs=[
            pl.BlockSpec((ROW_TILE, qd), lambda i: (i, 0)),
            pl.BlockSpec((ROW_TILE, kvd), lambda i: (i, 0)),
            pl.BlockSpec((ROW_TILE, kvd), lambda i: (i, 0)),
        ],
        out_shape=[
            jax.ShapeDtypeStruct((t, qd), BF16),
            jax.ShapeDtypeStruct((t, kvd), BF16),
            jax.ShapeDtypeStruct((t, kvd), BF16),
        ],
        compiler_params=_params(1),
        name="qkv_rope",
    )(x2d, g.reshape(1, -1), w_qkv.astype(BF16), b_qkv.reshape(1, -1), cos_l, sin_l)


def _attn_kernel(sink_ref, q_ref, k0_ref, k1_ref, k2_ref, k3_ref, v0_ref, v1_ref, v2_ref, v3_ref, o_ref,
                 *, n_blocks):
    blk = ATTN_BLOCK
    k_blocks = (k0_ref, k1_ref, k2_ref, k3_ref)
    v_blocks = (v0_ref, v1_ref, v2_ref, v3_ref)
    for sub in range(ATTN_Q_BLOCKS):
        jj = pl.program_id(1) * ATTN_Q_BLOCKS + sub
        kcat = jnp.concatenate([r[...] for r in k_blocks[sub:sub + 3]], axis=0)
        vcat = jnp.concatenate([r[...] for r in v_blocks[sub:sub + 3]], axis=0)
        _attn_block(sink_ref, q_ref, o_ref, kcat, vcat, jj, sub * blk, n_blocks)


def _attn_block(sink_ref, q_ref, o_ref, kcat, vcat, j, row0, n_blocks):
    blk = ATTN_BLOCK
    iq = lax.broadcasted_iota(I32, (blk, 3 * blk), 0)
    ik = lax.broadcasted_iota(I32, (blk, 3 * blk), 1)
    kpos = (j - 1) * blk + ik
    mask = (jnp.abs(iq + blk - ik) <= WINDOW) & (kpos >= 0) & (kpos < n_blocks * blk)
    mask = jnp.concatenate([mask] * Q_PER_KV, axis=0)
    lo_half = lax.broadcasted_iota(I32, (1, LANES), 1) < HEAD_DIM
    ones = jnp.ones((3 * blk, LANES), BF16)
    zero = jnp.zeros((), BF16)
    half = Q_PER_KV // 2 * blk
    rows = slice(row0, row0 + blk)
    for g in range(N_KV_HEADS):
        kd = kcat[:, g * LANES:(g + 1) * LANES]
        vd = vcat[:, g * LANES:(g + 1) * LANES]
        c0 = g * Q_PER_KV // 2
        qa = q_ref[rows, c0 * LANES:(c0 + 1) * LANES]
        qb = q_ref[rows, (c0 + 1) * LANES:(c0 + 2) * LANES]
        order = (0, 2, 1, 3)
        qs = jnp.concatenate([jnp.where(lo_half, qa, zero), jnp.where(lo_half, qb, zero),
                              jnp.where(lo_half, zero, qa), jnp.where(lo_half, zero, qb)], axis=0)
        s = _nt_dot(qs, kd)
        s = jnp.where(mask, s, NEG_INF)
        sink = jnp.concatenate([jnp.full((blk, 1), sink_ref[g * Q_PER_KV + u], F32) for u in order], axis=0)
        mx = jnp.maximum(jnp.max(s, axis=-1, keepdims=True), sink)
        p = jnp.exp(s - mx).astype(BF16)
        esink = jnp.exp(sink - mx)
        v_lo = jnp.concatenate([jnp.where(lo_half, vd, zero), ones], axis=1)
        v_hi = jnp.concatenate([jnp.where(lo_half, zero, vd), ones], axis=1)
        r_lo = jnp.dot(p[:half], v_lo, preferred_element_type=F32)
        r_hi = jnp.dot(p[half:], v_hi, preferred_element_type=F32)
        n_lo = r_lo[:, :LANES] / (r_lo[:, LANES:] + esink[:half])
        n_hi = r_hi[:, :LANES] / (r_hi[:, LANES:] + esink[half:])
        o_ref[rows, c0 * LANES:(c0 + 1) * LANES] = (n_lo[:blk] + n_hi[:blk]).astype(BF16)
        o_ref[rows, (c0 + 1) * LANES:(c0 + 2) * LANES] = (n_lo[blk:] + n_hi[blk:]).astype(BF16)


def _attention(q, k, v, sink, batch, seq):
    blk = ATTN_BLOCK
    nb = seq // blk
    nq = ATTN_Q_BLOCKS
    qd = q.shape[1]
    kvd = k.shape[1]

    def kv_spec(off):
        return pl.BlockSpec((blk, kvd), lambda b, j: (b * nb + jnp.clip(nq * j + off, 0, nb - 1), 0))

    kv_specs = [kv_spec(off) for off in range(-1, nq + 1)]
    q_spec = pl.BlockSpec((nq * blk, qd), lambda b, j: (b * (nb // nq) + j, 0))
    return pl.pallas_call(
        functools.partial(_attn_kernel, n_blocks=nb),
        grid=(batch, nb // nq),
        in_specs=[pl.BlockSpec(memory_space=pltpu.SMEM), q_spec] + kv_specs + kv_specs,
        out_specs=q_spec,
        out_shape=jax.ShapeDtypeStruct(q.shape, BF16),
        compiler_params=_params(2),
        name="band_attn",
    )(sink.astype(F32), q, *([k] * (nq + 2)), *([v] * (nq + 2)))


def _proj_res_kernel(a_ref, w_ref, b_ref, x_ref, o_ref):
    o_ref[...] = (jnp.dot(a_ref[...], w_ref[...], preferred_element_type=F32)
                  + b_ref[...] + x_ref[...])


def _proj_residual(a, w, b, x2d):
    t, d = x2d.shape
    return pl.pallas_call(
        _proj_res_kernel,
        grid=(t // ROW_TILE,),
        in_specs=[
            pl.BlockSpec((ROW_TILE, a.shape[1]), lambda i: (i, 0)),
            pl.BlockSpec(w.shape, lambda i: (0, 0)),
            pl.BlockSpec((1, d), lambda i: (0, 0)),
            pl.BlockSpec((ROW_TILE, d), lambda i: (i, 0)),
        ],
        out_specs=pl.BlockSpec((ROW_TILE, d), lambda i: (i, 0)),
        out_shape=jax.ShapeDtypeStruct(x2d.shape, F32),
        compiler_params=_params(1),
        name="proj_residual",
    )(a, w.astype(BF16), b.reshape(1, -1), x2d)


def _attention_layer(x2d, g, w_qkv, b_qkv, sink, w_o, b_o, batch, seq):
    q, k, v = _qkv_rope(x2d, g, w_qkv, b_qkv, seq)
    o = _attention(q, k, v, sink, batch, seq)
    return _proj_residual(o, w_o, b_o, x2d)


def _router_kernel(x_ref, g_ref, whi_ref, wlo_ref, rb_ref, ut_ref,
                   h_ref, e_ref, gate_ref, rank_ref, cnt_ref, carry_ref):
    i = pl.program_id(0)

    @pl.when(i == 0)
    def _():
        carry_ref[...] = jnp.zeros_like(carry_ref)

    h = _rms(x_ref[...], g_ref[...])
    for ch in range(ROW_CHUNKS):
        h_ref[pl.ds(ch, h.shape[0], stride=ROW_CHUNKS), :] = h[:, ch * LANES:(ch + 1) * LANES]
    h_hi = h.astype(BF16)
    h_lo = (h - h_hi.astype(F32)).astype(BF16)
    logits = (_nt_dot(whi_ref[...], h_hi) + _nt_dot(whi_ref[...], h_lo)
              + _nt_dot(wlo_ref[...], h_hi) + rb_ref[...])
    n_e, tm = logits.shape
    eid = lax.broadcasted_iota(I32, (n_e, tm), 0)
    vals = logits
    top_v, top_i, hots = [], [], []
    for _ in range(TOP_K):
        mx = jnp.max(vals, axis=0, keepdims=True)
        idx = jnp.min(jnp.where(vals == mx, eid, n_e), axis=0, keepdims=True)
        hot = eid == idx
        vals = jnp.where(hot, -jnp.inf, vals)
        top_v.append(mx)
        top_i.append(idx)
        hots.append(hot)
    ex = [jnp.exp(v - top_v[0]) for v in top_v]
    den = ex[0] + ex[1] + ex[2] + ex[3]
    member = (hots[0] | hots[1] | hots[2] | hots[3]).astype(F32)
    cum = jnp.dot(member.astype(BF16), ut_ref[...], preferred_element_type=F32)
    carry = carry_ref[:, 0:1]
    excl = carry + cum - member
    for kk in range(TOP_K):
        e_ref[kk:kk + 1, :] = top_i[kk]
        gate_ref[kk:kk + 1, :] = ex[kk] / den
        rank_ref[kk:kk + 1, :] = jnp.sum(jnp.where(hots[kk], excl, 0.0), axis=0, keepdims=True).astype(I32)
    new_carry = carry + jnp.sum(member, axis=1, keepdims=True)
    carry_ref[...] = jnp.broadcast_to(new_carry, carry_ref.shape)
    cnt_ref[...] = jnp.broadcast_to(new_carry, cnt_ref.shape)


def _router(x2d, g, router_w, router_b):
    t, d = x2d.shape
    tm = ROW_TILE
    wt = router_w.T.astype(F32)
    w_hi = wt.astype(BF16)
    w_lo = (wt - w_hi.astype(F32)).astype(BF16)
    ut = (jnp.arange(tm, dtype=I32)[:, None] <= jnp.arange(tm, dtype=I32)[None, :]).astype(BF16)
    kt_spec = pl.BlockSpec((TOP_K, tm), lambda i: (0, i))
    return pl.pallas_call(
        _router_kernel,
        grid=(t // tm,),
        in_specs=[
            pl.BlockSpec((tm, d), lambda i: (i, 0)),
            pl.BlockSpec((1, d), lambda i: (0, 0)),
            pl.BlockSpec((N_EXPERTS, d), lambda i: (0, 0)),
            pl.BlockSpec((N_EXPERTS, d), lambda i: (0, 0)),
            pl.BlockSpec((N_EXPERTS, 1), lambda i: (0, 0)),
            pl.BlockSpec((tm, tm), lambda i: (0, 0)),
        ],
        out_specs=[
            pl.BlockSpec((tm * ROW_CHUNKS, LANES), lambda i: (i, 0)),
            kt_spec, kt_spec, kt_spec,
            pl.BlockSpec((N_EXPERTS, LANES), lambda i: (0, 0)),
        ],
        out_shape=[
            jax.ShapeDtypeStruct((t * ROW_CHUNKS, LANES), F32),
            jax.ShapeDtypeStruct((TOP_K, t), I32),
            jax.ShapeDtypeStruct((TOP_K, t), F32),
            jax.ShapeDtypeStruct((TOP_K, t), I32),
            jax.ShapeDtypeStruct((N_EXPERTS, LANES), F32),
        ],
        scratch_shapes=[pltpu.VMEM((N_EXPERTS, LANES), F32)],
        compiler_params=_params(1),
        name="router",
    )(x2d, g.reshape(1, -1), w_hi, w_lo, router_b.reshape(-1, 1).astype(F32), ut)


def _expert_kernel(be_ref, base_ref, idx_hbm, h_hbm, wgu_ref, bgu_ref, wd_ref, bd_ref, y_ref,
                   idx0, idx1, xbuf0, xbuf1, wgu_bf, wd_bf, isem, gsem, *, n_blocks):
    i = pl.program_id(0)
    last_blk = n_blocks - 1
    bm = y_ref.shape[0] // ROW_CHUNKS
    d_ff = wd_bf.shape[0]
    idx_smem = (idx0, idx1)
    xbuf = (xbuf0, xbuf1)
    win = idx0.shape[0]
    nch = ROW_CHUNKS
    c = jnp.maximum(i - 1, 0)

    def idx_copy(blk, s):
        start = pl.multiple_of((base_ref[blk] // LANES) * LANES, LANES)
        return pltpu.make_async_copy(idx_hbm.at[pl.ds(start, win)], idx_smem[s], isem.at[s])

    def rows_done(s):
        return pltpu.make_async_copy(h_hbm.at[pl.ds(0, bm * nch), :], xbuf[s], gsem.at[s])

    @pl.when(i == 0)
    def _():
        idx_copy(0, 0).start()
        rows_done(1).start()

    @pl.when((i <= 1) | (be_ref[c] != be_ref[jnp.maximum(c - 1, 0)]))
    def _():
        wgu_bf[...] = wgu_ref[0, 0].astype(BF16)
        wd_bf[...] = wd_ref[0, 0].astype(BF16)

    def step(s):
        blk = jnp.minimum(i, last_blk)
        idx_copy(blk, s).wait()
        rows_done(1 - s).wait()
        idx_copy(jnp.minimum(i + 1, last_blk), 1 - s).start()
        off = lax.rem(base_ref[blk], LANES)
        for r in range(bm):
            tok = idx_smem[s][off + r]
            src = h_hbm.at[pl.ds(pl.multiple_of(tok * nch, nch), nch), :]
            pltpu.make_async_copy(src, xbuf[s].at[pl.ds(r * nch, nch), :], gsem.at[s]).start(priority=r % 2)
        x = jnp.concatenate([xbuf[1 - s][pl.ds(ch, bm, stride=nch), :] for ch in range(nch)], axis=1)
        gu = jnp.dot(x.astype(BF16), wgu_bf[...], preferred_element_type=F32) + bgu_ref[0, 0]
        gate = jnp.minimum(gu[:, :d_ff], SWIGLU_LIMIT)
        up = jnp.clip(gu[:, d_ff:], -SWIGLU_LIMIT, SWIGLU_LIMIT)
        glu = gate * jax.nn.sigmoid(gate * SWIGLU_ALPHA)
        act = ((up + 1.0) * glu).astype(BF16)
        y = jnp.dot(act, wd_bf[...], preferred_element_type=F32) + bd_ref[0, 0]
        for ch in range(nch):
            y_ref[pl.ds(ch, bm, stride=nch), :] = y[:, ch * LANES:(ch + 1) * LANES]

    for s in range(2):
        pl.when(lax.rem(i, 2) == s)(functools.partial(step, s))

    @pl.when(i == n_blocks)
    def _():
        s_last = n_blocks % 2
        idx_copy(last_blk, 1 - s_last).wait()
        rows_done(s_last).wait()


def _experts(h8, sorted_tok, block_expert, block_base, layer, w_gu, b_gu, w_down, b_down, n_blocks):
    bm = MOE_BM
    depth, n_e, d_ff, d = w_down.shape

    def cidx(i):
        return jnp.maximum(i - 1, 0)

    def wmap(i, be, base):
        return (layer, be[cidx(i)], 0, 0)

    grid_spec = pltpu.PrefetchScalarGridSpec(
        num_scalar_prefetch=2,
        grid=(n_blocks + 1,),
        in_specs=[
            pl.BlockSpec(memory_space=pl.ANY),
            pl.BlockSpec(memory_space=pl.ANY),
            pl.BlockSpec((1, 1, d, 2 * d_ff), wmap),
            pl.BlockSpec((1, 1, 1, 2 * d_ff), wmap),
            pl.BlockSpec((1, 1, d_ff, d), wmap),
            pl.BlockSpec((1, 1, 1, d), wmap),
        ],
        out_specs=pl.BlockSpec((bm * ROW_CHUNKS, LANES), lambda i, be, base: (cidx(i), 0)),
        scratch_shapes=[
            pltpu.SMEM((bm + LANES,), I32),
            pltpu.SMEM((bm + LANES,), I32),
            pltpu.VMEM((bm * ROW_CHUNKS, LANES), F32),
            pltpu.VMEM((bm * ROW_CHUNKS, LANES), F32),
            pltpu.VMEM((d, 2 * d_ff), BF16),
            pltpu.VMEM((d_ff, d), BF16),
            pltpu.SemaphoreType.DMA((2,)),
            pltpu.SemaphoreType.DMA((2,)),
        ],
    )
    return pl.pallas_call(
        functools.partial(_expert_kernel, n_blocks=n_blocks),
        grid_spec=grid_spec,
        out_shape=jax.ShapeDtypeStruct((n_blocks * bm * ROW_CHUNKS, LANES), F32),
        compiler_params=_params(1),
        name="moe_experts",
    )(block_expert, block_base, sorted_tok, h8, w_gu, b_gu.reshape(depth, n_e, 1, -1), w_down,
      b_down.reshape(depth, n_e, 1, -1))


def _combine_kernel(didx_hbm, ys_hbm, x_ref, gt_ref, fg_ref, o_ref, idx0, idx1, buf0, buf1, isem, gsem,
                    *, n_tiles, final_norm):
    i = pl.program_id(0)
    last = n_tiles - 1
    idx_smem = (idx0, idx1)
    buf = (buf0, buf1)
    rows = idx0.shape[0]
    tc = rows // TOP_K
    nch = ROW_CHUNKS

    def idx_copy(blk, s):
        return pltpu.make_async_copy(didx_hbm.at[pl.ds(blk * rows, rows)], idx_smem[s], isem.at[s])

    def rows_done(s):
        return pltpu.make_async_copy(ys_hbm.at[pl.ds(0, rows * nch), :], buf[s], gsem.at[s])

    @pl.when(i == 0)
    def _():
        idx_copy(0, 0).start()
        rows_done(1).start()

    def step(s):
        idx_copy(jnp.minimum(i, last), s).wait()
        rows_done(1 - s).wait()
        idx_copy(jnp.minimum(i + 1, last), 1 - s).start()
        for r in range(rows):
            src = ys_hbm.at[pl.ds(pl.multiple_of(idx_smem[s][r] * nch, nch), nch), :]
            pltpu.make_async_copy(src, buf[s].at[pl.ds(r * nch, nch), :], gsem.at[s]).start(priority=r % 2)
        acc = x_ref[...]
        gt = gt_ref[...]
        for kk in range(TOP_K):
            yk = jnp.concatenate([buf[1 - s][pl.ds(kk * tc * nch + ch, tc, stride=nch), :]
                                  for ch in range(nch)], axis=1)
            acc = acc + gt[:, kk:kk + 1] * yk
        if final_norm:
            acc = _rms(acc, fg_ref[...])
        o_ref[...] = acc

    for s in range(2):
        pl.when(lax.rem(i, 2) == s)(functools.partial(step, s))

    @pl.when(i == n_tiles)
    def _():
        s_last = n_tiles % 2
        idx_copy(last, 1 - s_last).wait()
        rows_done(s_last).wait()


def _combine(ys8, dest_tiles, gates_t, x2d, final_g):
    t, d = x2d.shape
    tc = COMBINE_TILE
    n = t // tc
    rows = TOP_K * tc

    def cidx(i):
        return jnp.maximum(i - 1, 0)

    fg = jnp.ones((1, d), F32) if final_g is None else final_g.reshape(1, -1).astype(F32)
    return pl.pallas_call(
        functools.partial(_combine_kernel, n_tiles=n, final_norm=final_g is not None),
        grid=(n + 1,),
        in_specs=[
            pl.BlockSpec(memory_space=pl.ANY),
            pl.BlockSpec(memory_space=pl.ANY),
            pl.BlockSpec((tc, d), lambda i: (cidx(i), 0)),
            pl.BlockSpec((tc, TOP_K), lambda i: (cidx(i), 0)),
            pl.BlockSpec((1, d), lambda i: (0, 0)),
        ],
        out_specs=pl.BlockSpec((tc, d), lambda i: (cidx(i), 0)),
        out_shape=jax.ShapeDtypeStruct((t, d), F32),
        scratch_shapes=[
            pltpu.SMEM((rows,), I32),
            pltpu.SMEM((rows,), I32),
            pltpu.VMEM((rows * ROW_CHUNKS, LANES), F32),
            pltpu.VMEM((rows * ROW_CHUNKS, LANES), F32),
            pltpu.SemaphoreType.DMA((2,)),
            pltpu.SemaphoreType.DMA((2,)),
        ],
        compiler_params=_params(1),
        name="moe_combine",
    )(dest_tiles, ys8, x2d, gates_t, fg)


def _moe_layer(x2d, g, router_w, router_b, layer, w_gu, b_gu, w_down, b_down, final_g):
    t, d = x2d.shape
    bm = MOE_BM
    a = t * TOP_K
    n_blocks = a // bm + N_EXPERTS

    h, e_kt, gate_kt, rank_kt, cnt = _router(x2d, g, router_w, router_b)

    counts = cnt[:, 0].astype(I32)
    blocks_e = (counts + bm - 1) // bm
    blk_end = jnp.cumsum(blocks_e)
    blk_start = blk_end - blocks_e
    n_valid = blk_end[-1]
    tok_start = jnp.cumsum(counts) - counts

    a_id = jnp.arange(t, dtype=I32)[None, :] * TOP_K + jnp.arange(TOP_K, dtype=I32)[:, None]
    sorted_key = jnp.sort((e_kt * a + a_id).reshape(-1))
    sorted_tok = jnp.pad((sorted_key % a) // TOP_K, (0, bm + LANES))

    eids = jnp.arange(N_EXPERTS, dtype=I32)
    bid = jnp.arange(n_blocks, dtype=I32)
    be = jnp.minimum(jnp.sum((bid[:, None] >= blk_end[None, :]).astype(I32), axis=1), N_EXPERTS - 1)
    hot = be[:, None] == eids[None, :]
    base = (jnp.sum(jnp.where(hot, tok_start[None, :], 0), axis=1)
            + (bid - jnp.sum(jnp.where(hot, blk_start[None, :], 0), axis=1)) * bm)
    valid = bid < n_valid
    last_e = jnp.sum(jnp.where(bid == n_valid - 1, be, 0))
    block_expert = jnp.where(valid, be, last_e).astype(I32)
    block_base = jnp.where(valid, base, 0).astype(I32)

    ys = _experts(h, sorted_tok, block_expert, block_base, layer, w_gu, b_gu, w_down, b_down, n_blocks)

    dest = rank_kt
    for e in range(N_EXPERTS):
        dest = dest + jnp.where(e_kt == e, blk_start[e] * bm, 0)
    tc = COMBINE_TILE
    dest_tiles = dest.reshape(TOP_K, t // tc, tc).transpose(1, 0, 2).reshape(-1).astype(I32)
    return _combine(ys, dest_tiles, gate_kt.T, x2d, final_g)


def kernel(x, norm_mix_g, norm_ffn_g, fnet_w_o, fnet_b_o, attn_w_qkv, attn_b_qkv, attn_sink, attn_w_o, attn_b_o, router_w, router_b, expert_w_gu, expert_b_gu, expert_w_down, expert_b_down, final_norm_g):
    batch, seq, d = x.shape
    depth = norm_mix_g.shape[0]
    x2d = x.reshape(batch * seq, d)
    for i in range(depth):
        j = i // 2
        if i % 2 == 0:
            x2d = _fourier_layer(x2d, norm_mix_g[i], fnet_w_o[j], fnet_b_o[j], batch, seq)
        else:
            x2d = _attention_layer(x2d, norm_mix_g[i], attn_w_qkv[j], attn_b_qkv[j], attn_sink[j],
                                   attn_w_o[j], attn_b_o[j], batch, seq)
        fg = final_norm_g if i == depth - 1 else None
        x2d = _moe_layer(x2d, norm_ffn_g[i], router_w[i], router_b[i], i, expert_w_gu, expert_b_gu,
                         expert_w_down, expert_b_down, fg)
    return x2d.reshape(batch, seq, d)
```

```python
import functools
import math

import jax
import jax.numpy as jnp
from jax import lax
from jax.experimental import pallas as pl
from jax.experimental.pallas import tpu as pltpu

F32 = jnp.float32
BF16 = jnp.bfloat16
I32 = jnp.int32

D_MODEL = 1024
FNET_GROUPS = 8
FNET_GROUP_DIM = D_MODEL // FNET_GROUPS
HEAD_DIM = 64
N_HEADS = 16
N_KV_HEADS = 4
Q_PER_KV = N_HEADS // N_KV_HEADS
KV_DIM = N_KV_HEADS * HEAD_DIM
WINDOW = 128
ATTN_BLOCK = 128
ROPE_THETA = 10000.0
N_EXPERTS = 32
TOP_K = 4
D_FF = D_MODEL
SWIGLU_ALPHA = 1.702
SWIGLU_LIMIT = 7.0
RMS_EPS = 1e-5
NEG_INF = -1e30

LANES = 128
ROW_CHUNKS = D_MODEL // LANES
VMEM_LIMIT = 56 * 1024 * 1024

ROW_TILE = 512
MOE_BM = 512
COMBINE_TILE = 128
ATTN_Q_BLOCKS = 2
DFT_CHAN_TILE = 256
DFT_TQ = 256
DFT_TK = 512

_ARB = pltpu.ARBITRARY


def _params(n_axes):
    return pltpu.CompilerParams(dimension_semantics=(_ARB,) * n_axes, vmem_limit_bytes=VMEM_LIMIT)


def _nt_dot(a, b):
    return lax.dot_general(a, b, (((1,), (1,)), ((), ())), preferred_element_type=F32)


def _rms(x, g):
    ms = jnp.mean(x * x, axis=-1, keepdims=True)
    return (x * lax.rsqrt(ms + RMS_EPS)) * g


DFT_RADIX = 4


def _fnet_chan_kernel(x0_ref, x1_ref, x2_ref, x3_ref, g_ref, cs_ref, a_ref, b_ref):
    hs = [_rms(xr[...], g_ref[...]).astype(BF16) for xr in (x0_ref, x1_ref, x2_ref, x3_ref)]
    gd = FNET_GROUP_DIM
    for gi in range(FNET_GROUPS):
        lo = gi * gd
        pq = [jnp.dot(h[:, lo:lo + gd], cs_ref[...], preferred_element_type=F32) for h in hs]
        p = [v[:, :gd] for v in pq]
        q = [v[:, gd:] for v in pq]
        dp02, dp13 = p[0] - p[2], p[1] - p[3]
        dq02, dq13 = q[0] - q[2], q[1] - q[3]
        sp02, sp13 = p[0] + p[2], p[1] + p[3]
        sq02, sq13 = q[0] + q[2], q[1] + q[3]
        a = (sp02 + sp13, dp02 - dq13, sp02 - sp13, dp02 + dq13)
        b = (-(sq02 + sq13), -dq02 - dp13, -(sq02 - sq13), dp13 - dq02)
        for r in range(DFT_RADIX):
            a_ref[r, :, lo:lo + gd] = a[r].astype(BF16)
            b_ref[r, :, lo:lo + gd] = b[r].astype(BF16)


def _fnet_chan(x2d, g, cs, batch, seq):
    t, d = x2d.shape
    qlen = seq // DFT_RADIX
    tm = DFT_CHAN_TILE
    per_q = qlen // tm

    def x_spec(j):
        return pl.BlockSpec((tm, d), lambda b, m: (b * (seq // tm) + j * per_q + m, 0))

    out_spec = pl.BlockSpec((DFT_RADIX, tm, d), lambda b, m: (0, b * per_q + m, 0))
    out_sds = jax.ShapeDtypeStruct((DFT_RADIX, batch * qlen, d), BF16)
    return pl.pallas_call(
        _fnet_chan_kernel,
        grid=(batch, per_q),
        in_specs=[x_spec(0), x_spec(1), x_spec(2), x_spec(3),
                  pl.BlockSpec((1, d), lambda b, m: (0, 0)),
                  pl.BlockSpec(cs.shape, lambda b, m: (0, 0))],
        out_specs=[out_spec, out_spec],
        out_shape=[out_sds, out_sds],
        compiler_params=_params(2),
        name="fnet_chan",
    )(x2d, x2d, x2d, x2d, g, cs)


def _fnet_seq_kernel(c_ref, s_ref, a_ref, b_ref, e_ref, wo_ref, bo_ref, x_ref, o_ref, acc_ref):
    k = pl.program_id(2)

    @pl.when(k == 0)
    def _():
        acc_ref[...] = jnp.zeros_like(acc_ref)

    for r in range(DFT_RADIX):
        acc_ref[r] += (jnp.dot(c_ref[r], a_ref[r], preferred_element_type=F32)
                       + jnp.dot(s_ref[r], b_ref[r], preferred_element_type=F32))

    @pl.when(k == pl.num_programs(2) - 1)
    def _():
        mixed = jnp.dot(e_ref[0], acc_ref[0].astype(BF16), preferred_element_type=F32)
        for r in range(1, DFT_RADIX):
            mixed += jnp.dot(e_ref[r], acc_ref[r].astype(BF16), preferred_element_type=F32)
        o_ref[...] = (jnp.dot(mixed.astype(BF16), wo_ref[...], preferred_element_type=F32)
                      + bo_ref[...] + x_ref[...])


def _fnet_seq(ctab, stab, a, b, wo, bo, x2d, batch, seq):
    d = x2d.shape[1]
    qlen = seq // DFT_RADIX
    tq, tk = DFT_TQ, DFT_TK
    qt, kt = qlen // tq, qlen // tk
    tab_spec = pl.BlockSpec((DFT_RADIX, tq, tk), lambda bb, i, k: (0, i, k))
    in_spec = pl.BlockSpec((DFT_RADIX, tk, d), lambda bb, i, k: (0, bb * kt + k, 0))
    row_spec = pl.BlockSpec((DFT_RADIX * tq, d), lambda bb, i, k: (bb * qt + i, 0))
    out_row = jnp.arange(DFT_RADIX * tq, dtype=I32)[None, :, None]
    src_row = jnp.arange(tq, dtype=I32)[None, None, :]
    res = jnp.arange(DFT_RADIX, dtype=I32)[:, None, None]
    expand = (out_row == DFT_RADIX * src_row + res).astype(BF16)
    return pl.pallas_call(
        _fnet_seq_kernel,
        grid=(batch, qt, kt),
        in_specs=[tab_spec, tab_spec, in_spec, in_spec,
                  pl.BlockSpec(expand.shape, lambda bb, i, k: (0, 0, 0)),
                  pl.BlockSpec((d, d), lambda bb, i, k: (0, 0)),
                  pl.BlockSpec((1, d), lambda bb, i, k: (0, 0)),
                  row_spec],
        out_specs=row_spec,
        out_shape=jax.ShapeDtypeStruct(x2d.shape, F32),
        scratch_shapes=[pltpu.VMEM((DFT_RADIX, tq, d), F32)],
        compiler_params=_params(3),
        name="fnet_seq",
    )(ctab, stab, a, b, expand, wo, bo, x2d)


def _dft_tables(seq):
    qlen = seq // DFT_RADIX
    m = jnp.arange(qlen, dtype=I32)
    freq = DFT_RADIX * m[None, :, None] + jnp.arange(DFT_RADIX, dtype=I32)[:, None, None]
    ang = ((freq * m[None, None, :]) % seq).astype(F32) * F32(2.0 * math.pi / seq)
    s_seq = F32(1.0 / math.sqrt(seq))
    ctab = (jnp.cos(ang) * s_seq).astype(BF16)
    stab = (jnp.sin(ang) * s_seq).astype(BF16)
    c = jnp.arange(FNET_GROUP_DIM, dtype=I32)
    cc = (c[:, None] * c[None, :]) % FNET_GROUP_DIM
    angc = cc.astype(F32) * F32(2.0 * math.pi / FNET_GROUP_DIM)
    s_ch = F32(1.0 / math.sqrt(FNET_GROUP_DIM))
    cs = jnp.concatenate([jnp.cos(angc) * s_ch, jnp.sin(angc) * s_ch], axis=1).astype(BF16)
    return ctab, stab, cs


def _fourier_layer(x2d, g, w_o, b_o, batch, seq):
    ctab, stab, cs = _dft_tables(seq)
    a, b = _fnet_chan(x2d, g.reshape(1, -1), cs, batch, seq)
    return _fnet_seq(ctab, stab, a, b, w_o.astype(BF16), b_o.reshape(1, -1), x2d, batch, seq)


def _dup_halves(x, lo_half):
    swapped = pltpu.roll(x, HEAD_DIM, axis=1)
    return jnp.where(lo_half, x, swapped), jnp.where(lo_half, swapped, x)


def _qkv_kernel(x_ref, g_ref, w_ref, b_ref, cos_ref, sin_ref, q_ref, k_ref, v_ref):
    h = _rms(x_ref[...], g_ref[...]).astype(BF16)
    qkv = jnp.dot(h, w_ref[...], preferred_element_type=F32) + b_ref[...]
    cos = cos_ref[...]
    sin = sin_ref[...]
    lane = lax.broadcasted_iota(I32, cos.shape, 1)
    first_half = (lane % HEAD_DIM) < (HEAD_DIM // 2)
    lo_half = lane < HEAD_DIM
    n_q = N_HEADS * HEAD_DIM // LANES
    n_kv = KV_DIM // LANES

    def rope(xc):
        fwd = pltpu.roll(xc, LANES - HEAD_DIM // 2, axis=1)
        bwd = pltpu.roll(xc, HEAD_DIM // 2, axis=1)
        return xc * cos + jnp.where(first_half, fwd, bwd) * sin

    for c in range(n_q):
        r = rope(qkv[:, c * LANES:(c + 1) * LANES])
        q_ref[:, c * LANES:(c + 1) * LANES] = (r * F32(HEAD_DIM ** -0.5)).astype(BF16)
    for c in range(n_kv):
        kc = rope(qkv[:, (n_q + c) * LANES:(n_q + c + 1) * LANES])
        vc = qkv[:, (n_q + n_kv + c) * LANES:(n_q + n_kv + c + 1) * LANES]
        for ref, val in ((k_ref, kc), (v_ref, vc)):
            a, b = _dup_halves(val, lo_half)
            ref[:, (2 * c) * LANES:(2 * c + 1) * LANES] = a.astype(BF16)
            ref[:, (2 * c + 1) * LANES:(2 * c + 2) * LANES] = b.astype(BF16)


def _qkv_rope(x2d, g, w_qkv, b_qkv, seq):
    t, d = x2d.shape
    qd = N_HEADS * HEAD_DIM
    kvd = N_KV_HEADS * LANES
    pos = jnp.arange(seq, dtype=F32)
    inv_freq = ROPE_THETA ** (-jnp.arange(0, HEAD_DIM, 2, dtype=F32) / HEAD_DIM)
    ang = pos[:, None] * inv_freq[None, :]
    cos = jnp.cos(ang)
    sin = jnp.sin(ang)
    reps = LANES // (HEAD_DIM // 2)
    cos_l = jnp.tile(cos, (1, reps))
    sin_l = jnp.tile(jnp.concatenate([-sin, sin], axis=1), (1, LANES // HEAD_DIM))
    spt = seq // ROW_TILE
    return pl.pallas_call(
        _qkv_kernel,
        grid=(t // ROW_TILE,),
        in_specs=[
            pl.BlockSpec((ROW_TILE, d), lambda i: (i, 0)),
            pl.BlockSpec((1, d), lambda i: (0, 0)),
            pl.BlockSpec(w_qkv.shape, lambda i: (0, 0)),
            pl.BlockSpec((1, w_qkv.shape[1]), lambda i: (0, 0)),
            pl.BlockSpec((ROW_TILE, LANES), lambda i: (i % spt, 0)),
            pl.BlockSpec((ROW_TILE, LANES), lambda i: (i % spt, 0)),
        ],
        out_specs=[
            pl.BlockSpec((ROW_TILE, qd), lambda i: (i, 0)),
            pl.BlockSpec((ROW_TILE, kvd), lambda i: (i, 0)),
            pl.BlockSpec((ROW_TILE, kvd), lambda i: (i, 0)),
        ],
        out_shape=[
            jax.ShapeDtypeStruct((t, qd), BF16),
            jax.ShapeDtypeStruct((t, kvd), BF16),
            jax.ShapeDtypeStruct((t, kvd), BF16),
        ],
        compiler_params=_params(1),
        name="qkv_rope",
    )(x2d, g.reshape(1, -1), w_qkv.astype(BF16), b_qkv.reshape(1, -1), cos_l, sin_l)


def _attn_kernel(sink_ref, q_ref, k0_ref, k1_ref, k2_ref, k3_ref, v0_ref, v1_ref, v2_ref, v3_ref,
                 wo_ref, bo_ref, x_ref, o_ref, *, n_blocks):
    blk = ATTN_BLOCK
    k_blocks = (k0_ref, k1_ref, k2_ref, k3_ref)
    v_blocks = (v0_ref, v1_ref, v2_ref, v3_ref)
    for sub in range(ATTN_Q_BLOCKS):
        jj = pl.program_id(1) * ATTN_Q_BLOCKS + sub
        kcat = jnp.concatenate([r[...] for r in k_blocks[sub:sub + 3]], axis=0)
        vcat = jnp.concatenate([r[...] for r in v_blocks[sub:sub + 3]], axis=0)
        rows = slice(sub * blk, (sub + 1) * blk)
        attn = _attn_block(sink_ref, q_ref, kcat, vcat, jj, rows, n_blocks)
        o_ref[rows, :] = (jnp.dot(attn, wo_ref[...], preferred_element_type=F32)
                          + bo_ref[...] + x_ref[rows, :])


def _attn_block(sink_ref, q_ref, kcat, vcat, j, rows, n_blocks):
    blk = ATTN_BLOCK
    iq = lax.broadcasted_iota(I32, (blk, 3 * blk), 0)
    ik = lax.broadcasted_iota(I32, (blk, 3 * blk), 1)
    kpos = (j - 1) * blk + ik
    mask = (jnp.abs(iq + blk - ik) <= WINDOW) & (kpos >= 0) & (kpos < n_blocks * blk)
    mask = jnp.concatenate([mask] * Q_PER_KV, axis=0)
    lo_half = lax.broadcasted_iota(I32, (1, LANES), 1) < HEAD_DIM
    ones = jnp.ones((3 * blk, LANES), BF16)
    zero = jnp.zeros((), BF16)
    half = Q_PER_KV // 2 * blk
    chunks = []
    for g in range(N_KV_HEADS):
        kd = kcat[:, g * LANES:(g + 1) * LANES]
        vd = vcat[:, g * LANES:(g + 1) * LANES]
        c0 = g * Q_PER_KV // 2
        qa = q_ref[rows, c0 * LANES:(c0 + 1) * LANES]
        qb = q_ref[rows, (c0 + 1) * LANES:(c0 + 2) * LANES]
        order = (0, 2, 1, 3)
        qs = jnp.concatenate([jnp.where(lo_half, qa, zero), jnp.where(lo_half, qb, zero),
                              jnp.where(lo_half, zero, qa), jnp.where(lo_half, zero, qb)], axis=0)
        s = _nt_dot(qs, kd)
        s = jnp.where(mask, s, NEG_INF)
        sink = jnp.concatenate([jnp.full((blk, 1), sink_ref[g * Q_PER_KV + u], F32) for u in order], axis=0)
        mx = jnp.maximum(jnp.max(s, axis=-1, keepdims=True), sink)
        p = jnp.exp(s - mx).astype(BF16)
        esink = jnp.exp(sink - mx)
        v_lo = jnp.concatenate([jnp.where(lo_half, vd, zero), ones], axis=1)
        v_hi = jnp.concatenate([jnp.where(lo_half, zero, vd), ones], axis=1)
        r_lo = jnp.dot(p[:half], v_lo, preferred_element_type=F32)
        r_hi = jnp.dot(p[half:], v_hi, preferred_element_type=F32)
        n_lo = r_lo[:, :LANES] / (r_lo[:, LANES:] + esink[:half])
        n_hi = r_hi[:, :LANES] / (r_hi[:, LANES:] + esink[half:])
        chunks.append((n_lo[:blk] + n_hi[:blk]).astype(BF16))
        chunks.append((n_lo[blk:] + n_hi[blk:]).astype(BF16))
    return jnp.concatenate(chunks, axis=1)


def _attention(q, k, v, sink, w_o, b_o, x2d, batch, seq):
    blk = ATTN_BLOCK
    nb = seq // blk
    nq = ATTN_Q_BLOCKS
    qd = q.shape[1]
    kvd = k.shape[1]
    d = x2d.shape[1]

    def kv_spec(off):
        return pl.BlockSpec((blk, kvd), lambda b, j: (b * nb + jnp.clip(nq * j + off, 0, nb - 1), 0))

    kv_specs = [kv_spec(off) for off in range(-1, nq + 1)]
    q_spec = pl.BlockSpec((nq * blk, qd), lambda b, j: (b * (nb // nq) + j, 0))
    x_spec = pl.BlockSpec((nq * blk, d), lambda b, j: (b * (nb // nq) + j, 0))
    return pl.pallas_call(
        functools.partial(_attn_kernel, n_blocks=nb),
        grid=(batch, nb // nq),
        in_specs=[pl.BlockSpec(memory_space=pltpu.SMEM), q_spec] + kv_specs + kv_specs + [
            pl.BlockSpec((qd, d), lambda b, j: (0, 0)),
            pl.BlockSpec((1, d), lambda b, j: (0, 0)),
            x_spec],
        out_specs=x_spec,
        out_shape=jax.ShapeDtypeStruct(x2d.shape, F32),
        compiler_params=_params(2),
        name="band_attn",
    )(sink.astype(F32), q, *([k] * (nq + 2)), *([v] * (nq + 2)), w_o.astype(BF16), b_o.reshape(1, -1), x2d)


def _attention_layer(x2d, g, w_qkv, b_qkv, sink, w_o, b_o, batch, seq):
    q, k, v = _qkv_rope(x2d, g, w_qkv, b_qkv, seq)
    return _attention(q, k, v, sink, w_o, b_o, x2d, batch, seq)


def _router_kernel(x_ref, g_ref, whi_ref, wlo_ref, rb_ref, ut_ref,
                   h_ref, e_ref, gate_ref, rank_ref, cnt_ref, carry_ref):
    i = pl.program_id(0)

    @pl.when(i == 0)
    def _():
        carry_ref[...] = jnp.zeros_like(carry_ref)

    h = _rms(x_ref[...], g_ref[...])
    for ch in range(ROW_CHUNKS):
        h_ref[pl.ds(ch, h.shape[0], stride=ROW_CHUNKS), :] = h[:, ch * LANES:(ch + 1) * LANES]
    h_hi = h.astype(BF16)
    h_lo = (h - h_hi.astype(F32)).astype(BF16)
    logits = (_nt_dot(whi_ref[...], h_hi) + _nt_dot(whi_ref[...], h_lo)
              + _nt_dot(wlo_ref[...], h_hi) + rb_ref[...])
    n_e, tm = logits.shape
    eid = lax.broadcasted_iota(I32, (n_e, tm), 0)
    vals = logits
    top_v, top_i, hots = [], [], []
    for _ in range(TOP_K):
        mx = jnp.max(vals, axis=0, keepdims=True)
        idx = jnp.min(jnp.where(vals == mx, eid, n_e), axis=0, keepdims=True)
        hot = eid == idx
        vals = jnp.where(hot, -jnp.inf, vals)
        top_v.append(mx)
        top_i.append(idx)
        hots.append(hot)
    ex = [jnp.exp(v - top_v[0]) for v in top_v]
    den = ex[0] + ex[1] + ex[2] + ex[3]
    member = (hots[0] | hots[1] | hots[2] | hots[3]).astype(F32)
    cum = jnp.dot(member.astype(BF16), ut_ref[...], preferred_element_type=F32)
    carry = carry_ref[:, 0:1]
    excl = carry + cum - member
    for kk in range(TOP_K):
        e_ref[kk:kk + 1, :] = top_i[kk]
        gate_ref[kk:kk + 1, :] = ex[kk] / den
        rank_ref[kk:kk + 1, :] = jnp.sum(jnp.where(hots[kk], excl, 0.0), axis=0, keepdims=True).astype(I32)
    new_carry = carry + jnp.sum(member, axis=1, keepdims=True)
    carry_ref[...] = jnp.broadcast_to(new_carry, carry_ref.shape)
    cnt_ref[...] = jnp.broadcast_to(new_carry, cnt_ref.shape)


def _router(x2d, g, router_w, router_b):
    t, d = x2d.shape
    tm = ROW_TILE
    wt = router_w.T.astype(F32)
    w_hi = wt.astype(BF16)
    w_lo = (wt - w_hi.astype(F32)).astype(BF16)
    ut = (jnp.arange(tm, dtype=I32)[:, None] <= jnp.arange(tm, dtype=I32)[None, :]).astype(BF16)
    kt_spec = pl.BlockSpec((TOP_K, tm), lambda i: (0, i))
    return pl.pallas_call(
        _router_kernel,
        grid=(t // tm,),
        in_specs=[
            pl.BlockSpec((tm, d), lambda i: (i, 0)),
            pl.BlockSpec((1, d), lambda i: (0, 0)),
            pl.BlockSpec((N_EXPERTS, d), lambda i: (0, 0)),
            pl.BlockSpec((N_EXPERTS, d), lambda i: (0, 0)),
            pl.BlockSpec((N_EXPERTS, 1), lambda i: (0, 0)),
            pl.BlockSpec((tm, tm), lambda i: (0, 0)),
        ],
        out_specs=[
            pl.BlockSpec((tm * ROW_CHUNKS, LANES), lambda i: (i, 0)),
            kt_spec, kt_spec, kt_spec,
            pl.BlockSpec((N_EXPERTS, LANES), lambda i: (0, 0)),
        ],
        out_shape=[
            jax.ShapeDtypeStruct((t * ROW_CHUNKS, LANES), F32),
            jax.ShapeDtypeStruct((TOP_K, t), I32),
            jax.ShapeDtypeStruct((TOP_K, t), F32),
            jax.ShapeDtypeStruct((TOP_K, t), I32),
            jax.ShapeDtypeStruct((N_EXPERTS, LANES), F32),
        ],
        scratch_shapes=[pltpu.VMEM((N_EXPERTS, LANES), F32)],
        compiler_params=_params(1),
        name="router",
    )(x2d, g.reshape(1, -1), w_hi, w_lo, router_b.reshape(-1, 1).astype(F32), ut)


def _expert_kernel(be_ref, base_ref, nv_ref, idx_hbm, h_hbm, wgu_ref, bgu_ref, wd_ref, bd_ref, y_ref,
                   idx0, idx1, xbuf0, xbuf1, wgu_bf, wd_bf, isem, gsem, *, n_blocks):
    i = pl.program_id(0)
    nv = nv_ref[0]
    last_blk = n_blocks - 1
    bm = y_ref.shape[0] // ROW_CHUNKS
    d_ff = wd_bf.shape[0]
    idx_smem = (idx0, idx1)
    xbuf = (xbuf0, xbuf1)
    win = idx0.shape[0]
    nch = ROW_CHUNKS
    c = jnp.maximum(i - 1, 0)

    def idx_copy(blk, s):
        start = pl.multiple_of((base_ref[blk] // LANES) * LANES, LANES)
        return pltpu.make_async_copy(idx_hbm.at[pl.ds(start, win)], idx_smem[s], isem.at[s])

    def rows_done(s):
        return pltpu.make_async_copy(h_hbm.at[pl.ds(0, bm * nch), :], xbuf[s], gsem.at[s])

    @pl.when(i == 0)
    def _():
        idx_copy(0, 0).start()
        rows_done(1).start()

    @pl.when((i <= 1) | (be_ref[c] != be_ref[jnp.maximum(c - 1, 0)]))
    def _():
        wgu_bf[...] = wgu_ref[0, 0].astype(BF16)
        wd_bf[...] = wd_ref[0, 0].astype(BF16)

    def step(s):
        blk = jnp.minimum(i, last_blk)
        idx_copy(blk, s).wait()
        rows_done(1 - s).wait()
        idx_copy(jnp.minimum(i + 1, last_blk), 1 - s).start()
        off = lax.rem(base_ref[blk], LANES)
        for r in range(bm):
            tok = idx_smem[s][off + r]
            src = h_hbm.at[pl.ds(pl.multiple_of(tok * nch, nch), nch), :]
            pltpu.make_async_copy(src, xbuf[s].at[pl.ds(r * nch, nch), :], gsem.at[s]).start(priority=r % 2)
        x = jnp.concatenate([xbuf[1 - s][pl.ds(ch, bm, stride=nch), :] for ch in range(nch)], axis=1)
        gu = jnp.dot(x.astype(BF16), wgu_bf[...], preferred_element_type=F32) + bgu_ref[0, 0]
        gate = jnp.minimum(gu[:, :d_ff], SWIGLU_LIMIT)
        up = jnp.clip(gu[:, d_ff:], -SWIGLU_LIMIT, SWIGLU_LIMIT)
        glu = gate * jax.nn.sigmoid(gate * SWIGLU_ALPHA)
        act = ((up + 1.0) * glu).astype(BF16)
        y = jnp.dot(act, wd_bf[...], preferred_element_type=F32) + bd_ref[0, 0]
        for ch in range(nch):
            y_ref[pl.ds(ch, bm, stride=nch), :] = y[:, ch * LANES:(ch + 1) * LANES]

    def drain(s):
        idx_copy(last_blk, 1 - s).wait()
        rows_done(s).wait()

    for s in range(2):
        pl.when((lax.rem(i, 2) == s) & (i <= nv))(functools.partial(step, s))
    for s in range(2):
        pl.when((lax.rem(i, 2) == s) & (i == nv))(functools.partial(drain, s))

    @pl.when(i > nv)
    def _():
        y_ref[...] = jnp.zeros_like(y_ref)


def _experts(h8, sorted_tok, block_expert, block_base, n_valid, layer, w_gu, b_gu, w_down, b_down, n_blocks):
    bm = MOE_BM
    depth, n_e, d_ff, d = w_down.shape

    def cidx(i):
        return jnp.maximum(i - 1, 0)

    def wmap(i, be, base, nv):
        return (layer, be[cidx(i)], 0, 0)

    grid_spec = pltpu.PrefetchScalarGridSpec(
        num_scalar_prefetch=3,
        grid=(n_blocks + 1,),
        in_specs=[
            pl.BlockSpec(memory_space=pl.ANY),
            pl.BlockSpec(memory_space=pl.ANY),
            pl.BlockSpec((1, 1, d, 2 * d_ff), wmap),
            pl.BlockSpec((1, 1, 1, 2 * d_ff), wmap),
            pl.BlockSpec((1, 1, d_ff, d), wmap),
            pl.BlockSpec((1, 1, 1, d), wmap),
        ],
        out_specs=pl.BlockSpec((bm * ROW_CHUNKS, LANES), lambda i, be, base, nv: (cidx(i), 0)),
        scratch_shapes=[
            pltpu.SMEM((bm + LANES,), I32),
            pltpu.SMEM((bm + LANES,), I32),
            pltpu.VMEM((bm * ROW_CHUNKS, LANES), F32),
            pltpu.VMEM((bm * ROW_CHUNKS, LANES), F32),
            pltpu.VMEM((d, 2 * d_ff), BF16),
            pltpu.VMEM((d_ff, d), BF16),
            pltpu.SemaphoreType.DMA((2,)),
            pltpu.SemaphoreType.DMA((2,)),
        ],
    )
    return pl.pallas_call(
        functools.partial(_expert_kernel, n_blocks=n_blocks),
        grid_spec=grid_spec,
        out_shape=jax.ShapeDtypeStruct((n_blocks * bm * ROW_CHUNKS, LANES), F32),
        compiler_params=_params(1),
        name="moe_experts",
    )(block_expert, block_base, n_valid, sorted_tok, h8, w_gu, b_gu.reshape(depth, n_e, 1, -1), w_down,
      b_down.reshape(depth, n_e, 1, -1))


def _combine_kernel(didx_hbm, ys_hbm, x_ref, gt_ref, fg_ref, o_ref, idx0, idx1, buf0, buf1, isem, gsem,
                    *, n_tiles, final_norm):
    i = pl.program_id(0)
    last = n_tiles - 1
    idx_smem = (idx0, idx1)
    buf = (buf0, buf1)
    rows = idx0.shape[0]
    tc = rows // TOP_K
    nch = ROW_CHUNKS

    def idx_copy(blk, s):
        return pltpu.make_async_copy(didx_hbm.at[pl.ds(blk * rows, rows)], idx_smem[s], isem.at[s])

    def rows_done(s):
        return pltpu.make_async_copy(ys_hbm.at[pl.ds(0, rows * nch), :], buf[s], gsem.at[s])

    @pl.when(i == 0)
    def _():
        idx_copy(0, 0).start()
        rows_done(1).start()

    def step(s):
        idx_copy(jnp.minimum(i, last), s).wait()
        rows_done(1 - s).wait()
        idx_copy(jnp.minimum(i + 1, last), 1 - s).start()
        for r in range(rows):
            src = ys_hbm.at[pl.ds(pl.multiple_of(idx_smem[s][r] * nch, nch), nch), :]
            pltpu.make_async_copy(src, buf[s].at[pl.ds(r * nch, nch), :], gsem.at[s]).start(priority=r % 2)
        acc = x_ref[...]
        gt = gt_ref[...]
        for kk in range(TOP_K):
            yk = jnp.concatenate([buf[1 - s][pl.ds(kk * tc * nch + ch, tc, stride=nch), :]
                                  for ch in range(nch)], axis=1)
            acc = acc + gt[:, kk:kk + 1] * yk
        if final_norm:
            acc = _rms(acc, fg_ref[...])
        o_ref[...] = acc

    for s in range(2):
        pl.when(lax.rem(i, 2) == s)(functools.partial(step, s))

    @pl.when(i == n_tiles)
    def _():
        s_last = n_tiles % 2
        idx_copy(last, 1 - s_last).wait()
        rows_done(s_last).wait()


def _combine(ys8, dest_tiles, gates_t, x2d, final_g):
    t, d = x2d.shape
    tc = COMBINE_TILE
    n = t // tc
    rows = TOP_K * tc

    def cidx(i):
        return jnp.maximum(i - 1, 0)

    fg = jnp.ones((1, d), F32) if final_g is None else final_g.reshape(1, -1).astype(F32)
    return pl.pallas_call(
        functools.partial(_combine_kernel, n_tiles=n, final_norm=final_g is not None),
        grid=(n + 1,),
        in_specs=[
            pl.BlockSpec(memory_space=pl.ANY),
            pl.BlockSpec(memory_space=pl.ANY),
            pl.BlockSpec((tc, d), lambda i: (cidx(i), 0)),
            pl.BlockSpec((tc, TOP_K), lambda i: (cidx(i), 0)),
            pl.BlockSpec((1, d), lambda i: (0, 0)),
        ],
        out_specs=pl.BlockSpec((tc, d), lambda i: (cidx(i), 0)),
        out_shape=jax.ShapeDtypeStruct((t, d), F32),
        scratch_shapes=[
            pltpu.SMEM((rows,), I32),
            pltpu.SMEM((rows,), I32),
            pltpu.VMEM((rows * ROW_CHUNKS, LANES), F32),
            pltpu.VMEM((rows * ROW_CHUNKS, LANES), F32),
            pltpu.SemaphoreType.DMA((2,)),
            pltpu.SemaphoreType.DMA((2,)),
        ],
        compiler_params=_params(1),
        name="moe_combine",
    )(dest_tiles, ys8, x2d, gates_t, fg)


def _moe_layer(x2d, g, router_w, router_b, layer, w_gu, b_gu, w_down, b_down, final_g):
    t, d = x2d.shape
    bm = MOE_BM
    a = t * TOP_K
    n_blocks = a // bm + N_EXPERTS

    h, e_kt, gate_kt, rank_kt, cnt = _router(x2d, g, router_w, router_b)

    counts = cnt[:, 0].astype(I32)
    blocks_e = (counts + bm - 1) // bm
    blk_end = jnp.cumsum(blocks_e)
    blk_start = blk_end - blocks_e
    n_valid = blk_end[-1]
    tok_start = jnp.cumsum(counts) - counts

    a_id = jnp.arange(t, dtype=I32)[None, :] * TOP_K + jnp.arange(TOP_K, dtype=I32)[:, None]
    sorted_key = jnp.sort((e_kt * a + a_id).reshape(-1))
    sorted_tok = jnp.pad((sorted_key % a) // TOP_K, (0, bm + LANES))

    eids = jnp.arange(N_EXPERTS, dtype=I32)
    bid = jnp.arange(n_blocks, dtype=I32)
    be = jnp.minimum(jnp.sum((bid[:, None] >= blk_end[None, :]).astype(I32), axis=1), N_EXPERTS - 1)
    hot = be[:, None] == eids[None, :]
    base = (jnp.sum(jnp.where(hot, tok_start[None, :], 0), axis=1)
            + (bid - jnp.sum(jnp.where(hot, blk_start[None, :], 0), axis=1)) * bm)
    valid = bid < n_valid
    last_e = jnp.sum(jnp.where(bid == n_valid - 1, be, 0))
    block_expert = jnp.where(valid, be, last_e).astype(I32)
    block_base = jnp.where(valid, base, 0).astype(I32)

    ys = _experts(h, sorted_tok, block_expert, block_base, n_valid.reshape(1).astype(I32), layer,
                  w_gu, b_gu, w_down, b_down, n_blocks)

    dest = rank_kt
    for e in range(N_EXPERTS):
        dest = dest + jnp.where(e_kt == e, blk_start[e] * bm, 0)
    tc = COMBINE_TILE
    dest_tiles = dest.reshape(TOP_K, t // tc, tc).transpose(1, 0, 2).reshape(-1).astype(I32)
    return _combine(ys, dest_tiles, gate_kt.T, x2d, final_g)


def kernel(x, norm_mix_g, norm_ffn_g, fnet_w_o, fnet_b_o, attn_w_qkv, attn_b_qkv, attn_sink, attn_w_o, attn_b_o, router_w, router_b, expert_w_gu, expert_b_gu, expert_w_down, expert_b_down, final_norm_g):
    batch, seq, d = x.shape
    depth = norm_mix_g.shape[0]
    x2d = x.reshape(batch * seq, d)
    for i in range(depth):
        j = i // 2
        if i % 2 == 0:
            x2d = _fourier_layer(x2d, norm_mix_g[i], fnet_w_o[j], fnet_b_o[j], batch, seq)
        else:
            x2d = _attention_layer(x2d, norm_mix_g[i], attn_w_qkv[j], attn_b_qkv[j], attn_sink[j],
                                   attn_w_o[j], attn_b_o[j], batch, seq)
        fg = final_norm_g if i == depth - 1 else None
        x2d = _moe_layer(x2d, norm_ffn_g[i], router_w[i], router_b[i], i, expert_w_gu, expert_b_gu,
                         expert_w_down, expert_b_down, fg)
    return x2d.reshape(batch, seq, d)
```

```python
import functools
import math

import jax
import jax.numpy as jnp
from jax import lax
from jax.experimental import pallas as pl
from jax.experimental.pallas import tpu as pltpu

F32 = jnp.float32
BF16 = jnp.bfloat16
I32 = jnp.int32

D_MODEL = 1024
FNET_GROUPS = 8
FNET_GROUP_DIM = D_MODEL // FNET_GROUPS
HEAD_DIM = 64
N_HEADS = 16
N_KV_HEADS = 4
Q_PER_KV = N_HEADS // N_KV_HEADS
KV_DIM = N_KV_HEADS * HEAD_DIM
WINDOW = 128
ATTN_BLOCK = 128
ROPE_THETA = 10000.0
N_EXPERTS = 32
TOP_K = 4
D_FF = D_MODEL
SWIGLU_ALPHA = 1.702
SWIGLU_LIMIT = 7.0
RMS_EPS = 1e-5
NEG_INF = -1e30

LANES = 128
ROW_CHUNKS = D_MODEL // LANES
VMEM_LIMIT = 56 * 1024 * 1024

ROW_TILE = 512
MOE_BM = 512
COMBINE_TILE = 128
ATTN_Q_BLOCKS = 2
DFT_CHAN_TILE = 256
DFT_TQ = 256
DFT_TK = 512

_ARB = pltpu.ARBITRARY


def _params(n_axes):
    return pltpu.CompilerParams(dimension_semantics=(_ARB,) * n_axes, vmem_limit_bytes=VMEM_LIMIT)


def _nt_dot(a, b):
    return lax.dot_general(a, b, (((1,), (1,)), ((), ())), preferred_element_type=F32)


def _rms(x, g):
    ms = jnp.mean(x * x, axis=-1, keepdims=True)
    return (x * lax.rsqrt(ms + RMS_EPS)) * g


DFT_RADIX = 4


def _fnet_chan_kernel(x0_ref, x1_ref, x2_ref, x3_ref, g_ref, cs_ref, a_ref, b_ref):
    hs = [_rms(xr[...], g_ref[...]).astype(BF16) for xr in (x0_ref, x1_ref, x2_ref, x3_ref)]
    gd = FNET_GROUP_DIM
    for gi in range(FNET_GROUPS):
        lo = gi * gd
        pq = [jnp.dot(h[:, lo:lo + gd], cs_ref[...], preferred_element_type=F32) for h in hs]
        p = [v[:, :gd] for v in pq]
        q = [v[:, gd:] for v in pq]
        dp02, dp13 = p[0] - p[2], p[1] - p[3]
        dq02, dq13 = q[0] - q[2], q[1] - q[3]
        sp02, sp13 = p[0] + p[2], p[1] + p[3]
        sq02, sq13 = q[0] + q[2], q[1] + q[3]
        a = (sp02 + sp13, dp02 - dq13, sp02 - sp13, dp02 + dq13)
        b = (-(sq02 + sq13), -dq02 - dp13, -(sq02 - sq13), dp13 - dq02)
        for r in range(DFT_RADIX):
            a_ref[r, :, lo:lo + gd] = a[r].astype(BF16)
            b_ref[r, :, lo:lo + gd] = b[r].astype(BF16)


def _fnet_chan(x2d, g, cs, batch, seq):
    t, d = x2d.shape
    qlen = seq // DFT_RADIX
    tm = DFT_CHAN_TILE
    per_q = qlen // tm

    def x_spec(j):
        return pl.BlockSpec((tm, d), lambda b, m: (b * (seq // tm) + j * per_q + m, 0))

    out_spec = pl.BlockSpec((DFT_RADIX, tm, d), lambda b, m: (0, b * per_q + m, 0))
    out_sds = jax.ShapeDtypeStruct((DFT_RADIX, batch * qlen, d), BF16)
    return pl.pallas_call(
        _fnet_chan_kernel,
        grid=(batch, per_q),
        in_specs=[x_spec(0), x_spec(1), x_spec(2), x_spec(3),
                  pl.BlockSpec((1, d), lambda b, m: (0, 0)),
                  pl.BlockSpec(cs.shape, lambda b, m: (0, 0))],
        out_specs=[out_spec, out_spec],
        out_shape=[out_sds, out_sds],
        compiler_params=_params(2),
        name="fnet_chan",
    )(x2d, x2d, x2d, x2d, g, cs)


def _fnet_seq_kernel(c_ref, s_ref, a_ref, b_ref, e_ref, wo_ref, bo_ref, x_ref, o_ref, acc_ref):
    k = pl.program_id(2)

    @pl.when(k == 0)
    def _():
        acc_ref[...] = jnp.zeros_like(acc_ref)

    for r in range(DFT_RADIX):
        acc_ref[r] += (jnp.dot(c_ref[r], a_ref[r], preferred_element_type=F32)
                       + jnp.dot(s_ref[r], b_ref[r], preferred_element_type=F32))

    @pl.when(k == pl.num_programs(2) - 1)
    def _():
        mixed = jnp.dot(e_ref[0], acc_ref[0].astype(BF16), preferred_element_type=F32)
        for r in range(1, DFT_RADIX):
            mixed += jnp.dot(e_ref[r], acc_ref[r].astype(BF16), preferred_element_type=F32)
        o_ref[...] = (jnp.dot(mixed.astype(BF16), wo_ref[...], preferred_element_type=F32)
                      + bo_ref[...] + x_ref[...])


def _fnet_seq(ctab, stab, a, b, wo, bo, x2d, batch, seq):
    d = x2d.shape[1]
    qlen = seq // DFT_RADIX
    tq, tk = DFT_TQ, DFT_TK
    qt, kt = qlen // tq, qlen // tk
    tab_spec = pl.BlockSpec((DFT_RADIX, tq, tk), lambda bb, i, k: (0, i, k))
    in_spec = pl.BlockSpec((DFT_RADIX, tk, d), lambda bb, i, k: (0, bb * kt + k, 0))
    row_spec = pl.BlockSpec((DFT_RADIX * tq, d), lambda bb, i, k: (bb * qt + i, 0))
    out_row = jnp.arange(DFT_RADIX * tq, dtype=I32)[None, :, None]
    src_row = jnp.arange(tq, dtype=I32)[None, None, :]
    res = jnp.arange(DFT_RADIX, dtype=I32)[:, None, None]
    expand = (out_row == DFT_RADIX * src_row + res).astype(BF16)
    return pl.pallas_call(
        _fnet_seq_kernel,
        grid=(batch, qt, kt),
        in_specs=[tab_spec, tab_spec, in_spec, in_spec,
                  pl.BlockSpec(expand.shape, lambda bb, i, k: (0, 0, 0)),
                  pl.BlockSpec((d, d), lambda bb, i, k: (0, 0)),
                  pl.BlockSpec((1, d), lambda bb, i, k: (0, 0)),
                  row_spec],
        out_specs=row_spec,
        out_shape=jax.ShapeDtypeStruct(x2d.shape, F32),
        scratch_shapes=[pltpu.VMEM((DFT_RADIX, tq, d), F32)],
        compiler_params=_params(3),
        name="fnet_seq",
    )(ctab, stab, a, b, expand, wo, bo, x2d)


def _dft_tables(seq):
    qlen = seq // DFT_RADIX
    m = jnp.arange(qlen, dtype=I32)
    freq = DFT_RADIX * m[None, :, None] + jnp.arange(DFT_RADIX, dtype=I32)[:, None, None]
    ang = ((freq * m[None, None, :]) % seq).astype(F32) * F32(2.0 * math.pi / seq)
    s_seq = F32(1.0 / math.sqrt(seq))
    ctab = (jnp.cos(ang) * s_seq).astype(BF16)
    stab = (jnp.sin(ang) * s_seq).astype(BF16)
    c = jnp.arange(FNET_GROUP_DIM, dtype=I32)
    cc = (c[:, None] * c[None, :]) % FNET_GROUP_DIM
    angc = cc.astype(F32) * F32(2.0 * math.pi / FNET_GROUP_DIM)
    s_ch = F32(1.0 / math.sqrt(FNET_GROUP_DIM))
    cs = jnp.concatenate([jnp.cos(angc) * s_ch, jnp.sin(angc) * s_ch], axis=1).astype(BF16)
    return ctab, stab, cs


def _fourier_layer(x2d, g, w_o, b_o, batch, seq):
    ctab, stab, cs = _dft_tables(seq)
    a, b = _fnet_chan(x2d, g.reshape(1, -1), cs, batch, seq)
    return _fnet_seq(ctab, stab, a, b, w_o.astype(BF16), b_o.reshape(1, -1), x2d, batch, seq)


def _dup_halves(x, lo_half):
    swapped = pltpu.roll(x, HEAD_DIM, axis=1)
    return jnp.where(lo_half, x, swapped), jnp.where(lo_half, swapped, x)


def _qkv_kernel(x_ref, g_ref, w_ref, b_ref, cos_ref, sin_ref, q_ref, k_ref, v_ref):
    h = _rms(x_ref[...], g_ref[...]).astype(BF16)
    qkv = jnp.dot(h, w_ref[...], preferred_element_type=F32) + b_ref[...]
    cos = cos_ref[...]
    sin = sin_ref[...]
    lane = lax.broadcasted_iota(I32, cos.shape, 1)
    first_half = (lane % HEAD_DIM) < (HEAD_DIM // 2)
    lo_half = lane < HEAD_DIM
    n_q = N_HEADS * HEAD_DIM // LANES
    n_kv = KV_DIM // LANES

    def rope(xc):
        fwd = pltpu.roll(xc, LANES - HEAD_DIM // 2, axis=1)
        bwd = pltpu.roll(xc, HEAD_DIM // 2, axis=1)
        return xc * cos + jnp.where(first_half, fwd, bwd) * sin

    for c in range(n_q):
        r = rope(qkv[:, c * LANES:(c + 1) * LANES])
        q_ref[:, c * LANES:(c + 1) * LANES] = (r * F32(HEAD_DIM ** -0.5)).astype(BF16)
    for c in range(n_kv):
        kc = rope(qkv[:, (n_q + c) * LANES:(n_q + c + 1) * LANES])
        vc = qkv[:, (n_q + n_kv + c) * LANES:(n_q + n_kv + c + 1) * LANES]
        for ref, val in ((k_ref, kc), (v_ref, vc)):
            a, b = _dup_halves(val, lo_half)
            ref[:, (2 * c) * LANES:(2 * c + 1) * LANES] = a.astype(BF16)
            ref[:, (2 * c + 1) * LANES:(2 * c + 2) * LANES] = b.astype(BF16)


def _qkv_rope(x2d, g, w_qkv, b_qkv, seq):
    t, d = x2d.shape
    qd = N_HEADS * HEAD_DIM
    kvd = N_KV_HEADS * LANES
    pos = jnp.arange(seq, dtype=F32)
    inv_freq = ROPE_THETA ** (-jnp.arange(0, HEAD_DIM, 2, dtype=F32) / HEAD_DIM)
    ang = pos[:, None] * inv_freq[None, :]
    cos = jnp.cos(ang)
    sin = jnp.sin(ang)
    reps = LANES // (HEAD_DIM // 2)
    cos_l = jnp.tile(cos, (1, reps))
    sin_l = jnp.tile(jnp.concatenate([-sin, sin], axis=1), (1, LANES // HEAD_DIM))
    spt = seq // ROW_TILE
    return pl.pallas_call(
        _qkv_kernel,
        grid=(t // ROW_TILE,),
        in_specs=[
            pl.BlockSpec((ROW_TILE, d), lambda i: (i, 0)),
            pl.BlockSpec((1, d), lambda i: (0, 0)),
            pl.BlockSpec(w_qkv.shape, lambda i: (0, 0)),
            pl.BlockSpec((1, w_qkv.shape[1]), lambda i: (0, 0)),
            pl.BlockSpec((ROW_TILE, LANES), lambda i: (i % spt, 0)),
            pl.BlockSpec((ROW_TILE, LANES), lambda i: (i % spt, 0)),
        ],
        out_specs=[
            pl.BlockSpec((ROW_TILE, qd), lambda i: (i, 0)),
            pl.BlockSpec((ROW_TILE, kvd), lambda i: (i, 0)),
            pl.BlockSpec((ROW_TILE, kvd), lambda i: (i, 0)),
        ],
        out_shape=[
            jax.ShapeDtypeStruct((t, qd), BF16),
            jax.ShapeDtypeStruct((t, kvd), BF16),
            jax.ShapeDtypeStruct((t, kvd), BF16),
        ],
        compiler_params=_params(1),
        name="qkv_rope",
    )(x2d, g.reshape(1, -1), w_qkv.astype(BF16), b_qkv.reshape(1, -1), cos_l, sin_l)


def _attn_kernel(sink_ref, q_ref, k0_ref, k1_ref, k2_ref, k3_ref, v0_ref, v1_ref, v2_ref, v3_ref,
                 wo_ref, bo_ref, x_ref, o_ref, *, n_blocks):
    blk = ATTN_BLOCK
    k_blocks = (k0_ref, k1_ref, k2_ref, k3_ref)
    v_blocks = (v0_ref, v1_ref, v2_ref, v3_ref)
    for sub in range(ATTN_Q_BLOCKS):
        jj = pl.program_id(1) * ATTN_Q_BLOCKS + sub
        kcat = jnp.concatenate([r[...] for r in k_blocks[sub:sub + 3]], axis=0)
        vcat = jnp.concatenate([r[...] for r in v_blocks[sub:sub + 3]], axis=0)
        rows = slice(sub * blk, (sub + 1) * blk)
        attn = _attn_block(sink_ref, q_ref, kcat, vcat, jj, rows, n_blocks)
        o_ref[rows, :] = (jnp.dot(attn, wo_ref[...], preferred_element_type=F32)
                          + bo_ref[...] + x_ref[rows, :])


def _attn_block(sink_ref, q_ref, kcat, vcat, j, rows, n_blocks):
    blk = ATTN_BLOCK
    iq = lax.broadcasted_iota(I32, (blk, 3 * blk), 0)
    ik = lax.broadcasted_iota(I32, (blk, 3 * blk), 1)
    kpos = (j - 1) * blk + ik
    mask = (jnp.abs(iq + blk - ik) <= WINDOW) & (kpos >= 0) & (kpos < n_blocks * blk)
    lo_half = lax.broadcasted_iota(I32, (1, LANES), 1) < HEAD_DIM
    ones = jnp.ones((3 * blk, LANES), BF16)
    zero = jnp.zeros((), BF16)
    chunks = []
    for g in range(N_KV_HEADS):
        kd = kcat[:, g * LANES:(g + 1) * LANES]
        vd = vcat[:, g * LANES:(g + 1) * LANES]
        v_half = (jnp.concatenate([jnp.where(lo_half, vd, zero), ones], axis=1),
                  jnp.concatenate([jnp.where(lo_half, zero, vd), ones], axis=1))
        for c in range(g * Q_PER_KV // 2, (g + 1) * Q_PER_KV // 2):
            qc = q_ref[rows, c * LANES:(c + 1) * LANES]
            out = None
            for hi in range(2):
                qh = jnp.where(lo_half, zero, qc) if hi else jnp.where(lo_half, qc, zero)
                s = jnp.where(mask, _nt_dot(qh, kd), NEG_INF)
                sink = sink_ref[2 * c + hi]
                mx = jnp.maximum(jnp.max(s, axis=-1, keepdims=True), sink)
                p = jnp.exp(s - mx).astype(BF16)
                r = jnp.dot(p, v_half[hi], preferred_element_type=F32)
                n = r[:, :LANES] / (r[:, LANES:] + jnp.exp(sink - mx))
                out = n if out is None else out + n
            chunks.append(out.astype(BF16))
    return jnp.concatenate(chunks, axis=1)


def _attention(q, k, v, sink, w_o, b_o, x2d, batch, seq):
    blk = ATTN_BLOCK
    nb = seq // blk
    nq = ATTN_Q_BLOCKS
    qd = q.shape[1]
    kvd = k.shape[1]
    d = x2d.shape[1]

    def kv_spec(off):
        return pl.BlockSpec((blk, kvd), lambda b, j: (b * nb + jnp.clip(nq * j + off, 0, nb - 1), 0))

    kv_specs = [kv_spec(off) for off in range(-1, nq + 1)]
    q_spec = pl.BlockSpec((nq * blk, qd), lambda b, j: (b * (nb // nq) + j, 0))
    x_spec = pl.BlockSpec((nq * blk, d), lambda b, j: (b * (nb // nq) + j, 0))
    return pl.pallas_call(
        functools.partial(_attn_kernel, n_blocks=nb),
        grid=(batch, nb // nq),
        in_specs=[pl.BlockSpec(memory_space=pltpu.SMEM), q_spec] + kv_specs + kv_specs + [
            pl.BlockSpec((qd, d), lambda b, j: (0, 0)),
            pl.BlockSpec((1, d), lambda b, j: (0, 0)),
            x_spec],
        out_specs=x_spec,
        out_shape=jax.ShapeDtypeStruct(x2d.shape, F32),
        compiler_params=_params(2),
        name="band_attn",
    )(sink.astype(F32), q, *([k] * (nq + 2)), *([v] * (nq + 2)), w_o.astype(BF16), b_o.reshape(1, -1), x2d)


def _attention_layer(x2d, g, w_qkv, b_qkv, sink, w_o, b_o, batch, seq):
    q, k, v = _qkv_rope(x2d, g, w_qkv, b_qkv, seq)
    return _attention(q, k, v, sink, w_o, b_o, x2d, batch, seq)


def _router_kernel(x_ref, g_ref, whi_ref, wlo_ref, rb_ref, ut_ref,
                   h_ref, e_ref, gate_ref, rank_ref, cnt_ref, carry_ref):
    i = pl.program_id(0)

    @pl.when(i == 0)
    def _():
        carry_ref[...] = jnp.zeros_like(carry_ref)

    h = _rms(x_ref[...], g_ref[...])
    for ch in range(ROW_CHUNKS):
        h_ref[pl.ds(ch, h.shape[0], stride=ROW_CHUNKS), :] = h[:, ch * LANES:(ch + 1) * LANES]
    h_hi = h.astype(BF16)
    h_lo = (h - h_hi.astype(F32)).astype(BF16)
    logits = (_nt_dot(whi_ref[...], h_hi) + _nt_dot(whi_ref[...], h_lo)
              + _nt_dot(wlo_ref[...], h_hi) + rb_ref[...])
    n_e, tm = logits.shape
    eid = lax.broadcasted_iota(I32, (n_e, tm), 0)
    vals = logits
    top_v, top_i, hots = [], [], []
    for _ in range(TOP_K):
        mx = jnp.max(vals, axis=0, keepdims=True)
        idx = jnp.min(jnp.where(vals == mx, eid, n_e), axis=0, keepdims=True)
        hot = eid == idx
        vals = jnp.where(hot, -jnp.inf, vals)
        top_v.append(mx)
        top_i.append(idx)
        hots.append(hot)
    ex = [jnp.exp(v - top_v[0]) for v in top_v]
    den = ex[0] + ex[1] + ex[2] + ex[3]
    member = (hots[0] | hots[1] | hots[2] | hots[3]).astype(F32)
    cum = jnp.dot(member.astype(BF16), ut_ref[...], preferred_element_type=F32)
    carry = carry_ref[:, 0:1]
    excl = carry + cum - member
    for kk in range(TOP_K):
        e_ref[kk:kk + 1, :] = top_i[kk]
        gate_ref[kk:kk + 1, :] = ex[kk] / den
        rank_ref[kk:kk + 1, :] = jnp.sum(jnp.where(hots[kk], excl, 0.0), axis=0, keepdims=True).astype(I32)
    new_carry = carry + jnp.sum(member, axis=1, keepdims=True)
    carry_ref[...] = jnp.broadcast_to(new_carry, carry_ref.shape)
    cnt_ref[...] = jnp.broadcast_to(new_carry, cnt_ref.shape)


def _router(x2d, g, router_w, router_b):
    t, d = x2d.shape
    tm = ROW_TILE
    wt = router_w.T.astype(F32)
    w_hi = wt.astype(BF16)
    w_lo = (wt - w_hi.astype(F32)).astype(BF16)
    ut = (jnp.arange(tm, dtype=I32)[:, None] <= jnp.arange(tm, dtype=I32)[None, :]).astype(BF16)
    kt_spec = pl.BlockSpec((TOP_K, tm), lambda i: (0, i))
    return pl.pallas_call(
        _router_kernel,
        grid=(t // tm,),
        in_specs=[
            pl.BlockSpec((tm, d), lambda i: (i, 0)),
            pl.BlockSpec((1, d), lambda i: (0, 0)),
            pl.BlockSpec((N_EXPERTS, d), lambda i: (0, 0)),
            pl.BlockSpec((N_EXPERTS, d), lambda i: (0, 0)),
            pl.BlockSpec((N_EXPERTS, 1), lambda i: (0, 0)),
            pl.BlockSpec((tm, tm), lambda i: (0, 0)),
        ],
        out_specs=[
            pl.BlockSpec((tm * ROW_CHUNKS, LANES), lambda i: (i, 0)),
            kt_spec, kt_spec, kt_spec,
            pl.BlockSpec((N_EXPERTS, LANES), lambda i: (0, 0)),
        ],
        out_shape=[
            jax.ShapeDtypeStruct((t * ROW_CHUNKS, LANES), F32),
            jax.ShapeDtypeStruct((TOP_K, t), I32),
            jax.ShapeDtypeStruct((TOP_K, t), F32),
            jax.ShapeDtypeStruct((TOP_K, t), I32),
            jax.ShapeDtypeStruct((N_EXPERTS, LANES), F32),
        ],
        scratch_shapes=[pltpu.VMEM((N_EXPERTS, LANES), F32)],
        compiler_params=_params(1),
        name="router",
    )(x2d, g.reshape(1, -1), w_hi, w_lo, router_b.reshape(-1, 1).astype(F32), ut)


def _expert_kernel(be_ref, base_ref, nv_ref, idx_hbm, h_hbm, wgu_ref, bgu_ref, wd_ref, bd_ref, y_ref,
                   idx0, idx1, xbuf0, xbuf1, wgu_bf, wd_bf, isem, gsem, *, n_blocks):
    i = pl.program_id(0)
    nv = nv_ref[0]
    last_blk = n_blocks - 1
    bm = y_ref.shape[0] // ROW_CHUNKS
    d_ff = wd_bf.shape[0]
    idx_smem = (idx0, idx1)
    xbuf = (xbuf0, xbuf1)
    win = idx0.shape[0]
    nch = ROW_CHUNKS
    c = jnp.maximum(i - 1, 0)

    def idx_copy(blk, s):
        start = pl.multiple_of((base_ref[blk] // LANES) * LANES, LANES)
        return pltpu.make_async_copy(idx_hbm.at[pl.ds(start, win)], idx_smem[s], isem.at[s])

    def rows_done(s):
        return pltpu.make_async_copy(h_hbm.at[pl.ds(0, bm * nch), :], xbuf[s], gsem.at[s])

    @pl.when(i == 0)
    def _():
        idx_copy(0, 0).start()
        rows_done(1).start()

    @pl.when((i <= 1) | (be_ref[c] != be_ref[jnp.maximum(c - 1, 0)]))
    def _():
        wgu_bf[...] = wgu_ref[0, 0].astype(BF16)
        wd_bf[...] = wd_ref[0, 0].astype(BF16)

    def step(s):
        blk = jnp.minimum(i, last_blk)
        idx_copy(blk, s).wait()
        rows_done(1 - s).wait()
        idx_copy(jnp.minimum(i + 1, last_blk), 1 - s).start()
        off = lax.rem(base_ref[blk], LANES)
        for r in range(bm):
            tok = idx_smem[s][off + r]
            src = h_hbm.at[pl.ds(pl.multiple_of(tok * nch, nch), nch), :]
            pltpu.make_async_copy(src, xbuf[s].at[pl.ds(r * nch, nch), :], gsem.at[s]).start(priority=r % 2)
        x = jnp.concatenate([xbuf[1 - s][pl.ds(ch, bm, stride=nch), :] for ch in range(nch)], axis=1)
        gu = jnp.dot(x.astype(BF16), wgu_bf[...], preferred_element_type=F32) + bgu_ref[0, 0]
        gate = jnp.minimum(gu[:, :d_ff], SWIGLU_LIMIT)
        up = jnp.clip(gu[:, d_ff:], -SWIGLU_LIMIT, SWIGLU_LIMIT)
        glu = gate * jax.nn.sigmoid(gate * SWIGLU_ALPHA)
        act = ((up + 1.0) * glu).astype(BF16)
        y = jnp.dot(act, wd_bf[...], preferred_element_type=F32) + bd_ref[0, 0]
        for ch in range(nch):
            y_ref[pl.ds(ch, bm, stride=nch), :] = y[:, ch * LANES:(ch + 1) * LANES]

    def drain(s):
        idx_copy(last_blk, 1 - s).wait()
        rows_done(s).wait()

    for s in range(2):
        pl.when((lax.rem(i, 2) == s) & (i <= nv))(functools.partial(step, s))
    for s in range(2):
        pl.when((lax.rem(i, 2) == s) & (i == nv))(functools.partial(drain, s))

    @pl.when(i > nv)
    def _():
        y_ref[...] = jnp.zeros_like(y_ref)


def _experts(h8, sorted_tok, block_expert, block_base, n_valid, layer, w_gu, b_gu, w_down, b_down, n_blocks):
    bm = MOE_BM
    depth, n_e, d_ff, d = w_down.shape

    def cidx(i):
        return jnp.maximum(i - 1, 0)

    def wmap(i, be, base, nv):
        return (layer, be[cidx(i)], 0, 0)

    grid_spec = pltpu.PrefetchScalarGridSpec(
        num_scalar_prefetch=3,
        grid=(n_blocks + 1,),
        in_specs=[
            pl.BlockSpec(memory_space=pl.ANY),
            pl.BlockSpec(memory_space=pl.ANY),
            pl.BlockSpec((1, 1, d, 2 * d_ff), wmap),
            pl.BlockSpec((1, 1, 1, 2 * d_ff), wmap),
            pl.BlockSpec((1, 1, d_ff, d), wmap),
            pl.BlockSpec((1, 1, 1, d), wmap),
        ],
        out_specs=pl.BlockSpec((bm * ROW_CHUNKS, LANES), lambda i, be, base, nv: (cidx(i), 0)),
        scratch_shapes=[
            pltpu.SMEM((bm + LANES,), I32),
            pltpu.SMEM((bm + LANES,), I32),
            pltpu.VMEM((bm * ROW_CHUNKS, LANES), F32),
            pltpu.VMEM((bm * ROW_CHUNKS, LANES), F32),
            pltpu.VMEM((d, 2 * d_ff), BF16),
            pltpu.VMEM((d_ff, d), BF16),
            pltpu.SemaphoreType.DMA((2,)),
            pltpu.SemaphoreType.DMA((2,)),
        ],
    )
    return pl.pallas_call(
        functools.partial(_expert_kernel, n_blocks=n_blocks),
        grid_spec=grid_spec,
        out_shape=jax.ShapeDtypeStruct((n_blocks * bm * ROW_CHUNKS, LANES), F32),
        compiler_params=_params(1),
        name="moe_experts",
    )(block_expert, block_base, n_valid, sorted_tok, h8, w_gu, b_gu.reshape(depth, n_e, 1, -1), w_down,
      b_down.reshape(depth, n_e, 1, -1))


def _combine_kernel(didx_hbm, ys_hbm, x_ref, gt_ref, fg_ref, o_ref, idx0, idx1, buf0, buf1, isem, gsem,
                    *, n_tiles, final_norm):
    i = pl.program_id(0)
    last = n_tiles - 1
    idx_smem = (idx0, idx1)
    buf = (buf0, buf1)
    rows = idx0.shape[0]
    tc = rows // TOP_K
    nch = ROW_CHUNKS

    def idx_copy(blk, s):
        return pltpu.make_async_copy(didx_hbm.at[pl.ds(blk * rows, rows)], idx_smem[s], isem.at[s])

    def rows_done(s):
        return pltpu.make_async_copy(ys_hbm.at[pl.ds(0, rows * nch), :], buf[s], gsem.at[s])

    @pl.when(i == 0)
    def _():
        idx_copy(0, 0).start()
        rows_done(1).start()

    def step(s):
        idx_copy(jnp.minimum(i, last), s).wait()
        rows_done(1 - s).wait()
        idx_copy(jnp.minimum(i + 1, last), 1 - s).start()
        for r in range(rows):
            src = ys_hbm.at[pl.ds(pl.multiple_of(idx_smem[s][r] * nch, nch), nch), :]
            pltpu.make_async_copy(src, buf[s].at[pl.ds(r * nch, nch), :], gsem.at[s]).start(priority=r % 2)
        acc = x_ref[...]
        gt = gt_ref[...]
        for kk in range(TOP_K):
            yk = jnp.concatenate([buf[1 - s][pl.ds(kk * tc * nch + ch, tc, stride=nch), :]
                                  for ch in range(nch)], axis=1)
            acc = acc + gt[:, kk:kk + 1] * yk
        if final_norm:
            acc = _rms(acc, fg_ref[...])
        o_ref[...] = acc

    for s in range(2):
        pl.when(lax.rem(i, 2) == s)(functools.partial(step, s))

    @pl.when(i == n_tiles)
    def _():
        s_last = n_tiles % 2
        idx_copy(last, 1 - s_last).wait()
        rows_done(s_last).wait()


def _combine(ys8, dest_tiles, gates_t, x2d, final_g):
    t, d = x2d.shape
    tc = COMBINE_TILE
    n = t // tc
    rows = TOP_K * tc

    def cidx(i):
        return jnp.maximum(i - 1, 0)

    fg = jnp.ones((1, d), F32) if final_g is None else final_g.reshape(1, -1).astype(F32)
    return pl.pallas_call(
        functools.partial(_combine_kernel, n_tiles=n, final_norm=final_g is not None),
        grid=(n + 1,),
        in_specs=[
            pl.BlockSpec(memory_space=pl.ANY),
            pl.BlockSpec(memory_space=pl.ANY),
            pl.BlockSpec((tc, d), lambda i: (cidx(i), 0)),
            pl.BlockSpec((tc, TOP_K), lambda i: (cidx(i), 0)),
            pl.BlockSpec((1, d), lambda i: (0, 0)),
        ],
        out_specs=pl.BlockSpec((tc, d), lambda i: (cidx(i), 0)),
        out_shape=jax.ShapeDtypeStruct((t, d), F32),
        scratch_shapes=[
            pltpu.SMEM((rows,), I32),
            pltpu.SMEM((rows,), I32),
            pltpu.VMEM((rows * ROW_CHUNKS, LANES), F32),
            pltpu.VMEM((rows * ROW_CHUNKS, LANES), F32),
            pltpu.SemaphoreType.DMA((2,)),
            pltpu.SemaphoreType.DMA((2,)),
        ],
        compiler_params=_params(1),
        name="moe_combine",
    )(dest_tiles, ys8, x2d, gates_t, fg)


def _moe_layer(x2d, g, router_w, router_b, layer, w_gu, b_gu, w_down, b_down, final_g):
    t, d = x2d.shape
    bm = MOE_BM
    a = t * TOP_K
    n_blocks = a // bm + N_EXPERTS

    h, e_kt, gate_kt, rank_kt, cnt = _router(x2d, g, router_w, router_b)

    counts = cnt[:, 0].astype(I32)
    blocks_e = (counts + bm - 1) // bm
    blk_end = jnp.cumsum(blocks_e)
    blk_start = blk_end - blocks_e
    n_valid = blk_end[-1]
    tok_start = jnp.cumsum(counts) - counts

    a_id = jnp.arange(t, dtype=I32)[None, :] * TOP_K + jnp.arange(TOP_K, dtype=I32)[:, None]
    sorted_key = jnp.sort((e_kt * a + a_id).reshape(-1))
    sorted_tok = jnp.pad((sorted_key % a) // TOP_K, (0, bm + LANES))

    eids = jnp.arange(N_EXPERTS, dtype=I32)
    bid = jnp.arange(n_blocks, dtype=I32)
    be = jnp.minimum(jnp.sum((bid[:, None] >= blk_end[None, :]).astype(I32), axis=1), N_EXPERTS - 1)
    hot = be[:, None] == eids[None, :]
    base = (jnp.sum(jnp.where(hot, tok_start[None, :], 0), axis=1)
            + (bid - jnp.sum(jnp.where(hot, blk_start[None, :], 0), axis=1)) * bm)
    valid = bid < n_valid
    last_e = jnp.sum(jnp.where(bid == n_valid - 1, be, 0))
    block_expert = jnp.where(valid, be, last_e).astype(I32)
    block_base = jnp.where(valid, base, 0).astype(I32)

    ys = _experts(h, sorted_tok, block_expert, block_base, n_valid.reshape(1).astype(I32), layer,
                  w_gu, b_gu, w_down, b_down, n_blocks)

    dest = rank_kt
    for e in range(N_EXPERTS):
        dest = dest + jnp.where(e_kt == e, blk_start[e] * bm, 0)
    tc = COMBINE_TILE
    dest_tiles = dest.reshape(TOP_K, t // tc, tc).transpose(1, 0, 2).reshape(-1).astype(I32)
    return _combine(ys, dest_tiles, gate_kt.T, x2d, final_g)


def kernel(x, norm_mix_g, norm_ffn_g, fnet_w_o, fnet_b_o, attn_w_qkv, attn_b_qkv, attn_sink, attn_w_o, attn_b_o, router_w, router_b, expert_w_gu, expert_b_gu, expert_w_down, expert_b_down, final_norm_g):
    batch, seq, d = x.shape
    depth = norm_mix_g.shape[0]
    x2d = x.reshape(batch * seq, d)
    for i in range(depth):
        j = i // 2
        if i % 2 == 0:
            x2d = _fourier_layer(x2d, norm_mix_g[i], fnet_w_o[j], fnet_b_o[j], batch, seq)
        else:
            x2d = _attention_layer(x2d, norm_mix_g[i], attn_w_qkv[j], attn_b_qkv[j], attn_sink[j],
                                   attn_w_o[j], attn_b_o[j], batch, seq)
        fg = final_norm_g if i == depth - 1 else None
        x2d = _moe_layer(x2d, norm_ffn_g[i], router_w[i], router_b[i], i, expert_w_gu, expert_b_gu,
                         expert_w_down, expert_b_down, fg)
    return x2d.reshape(batch, seq, d)
```

```python
import functools
import math

import jax
import jax.numpy as jnp
from jax import lax
from jax.experimental import pallas as pl
from jax.experimental.pallas import tpu as pltpu

F32 = jnp.float32
BF16 = jnp.bfloat16
I32 = jnp.int32

D_MODEL = 1024
FNET_GROUPS = 8
FNET_GROUP_DIM = D_MODEL // FNET_GROUPS
HEAD_DIM = 64
N_HEADS = 16
N_KV_HEADS = 4
Q_PER_KV = N_HEADS // N_KV_HEADS
KV_DIM = N_KV_HEADS * HEAD_DIM
WINDOW = 128
ATTN_BLOCK = 128
ROPE_THETA = 10000.0
N_EXPERTS = 32
TOP_K = 4
D_FF = D_MODEL
SWIGLU_ALPHA = 1.702
SWIGLU_LIMIT = 7.0
RMS_EPS = 1e-5
NEG_INF = -1e30

LANES = 128
ROW_CHUNKS = D_MODEL // LANES
VMEM_LIMIT = 56 * 1024 * 1024

ROW_TILE = 1024
MOE_BM = 512
COMBINE_TILE = 128
ATTN_Q_BLOCKS = 4
DFT_CHAN_TILE = 256
DFT_TQ = 256
DFT_TK = 512

_ARB = pltpu.ARBITRARY


def _params(n_axes):
    return pltpu.CompilerParams(dimension_semantics=(_ARB,) * n_axes, vmem_limit_bytes=VMEM_LIMIT)


def _nt_dot(a, b):
    return lax.dot_general(a, b, (((1,), (1,)), ((), ())), preferred_element_type=F32)


def _rms(x, g):
    ms = jnp.mean(x * x, axis=-1, keepdims=True)
    return (x * lax.rsqrt(ms + RMS_EPS)) * g


DFT_RADIX = 4


def _fnet_chan_kernel(x0_ref, x1_ref, x2_ref, x3_ref, g_ref, cs_ref, a_ref, b_ref):
    hs = [_rms(xr[...], g_ref[...]).astype(BF16) for xr in (x0_ref, x1_ref, x2_ref, x3_ref)]
    gd = FNET_GROUP_DIM
    for gi in range(FNET_GROUPS):
        lo = gi * gd
        pq = [jnp.dot(h[:, lo:lo + gd], cs_ref[...], preferred_element_type=F32) for h in hs]
        p = [v[:, :gd] for v in pq]
        q = [v[:, gd:] for v in pq]
        dp02, dp13 = p[0] - p[2], p[1] - p[3]
        dq02, dq13 = q[0] - q[2], q[1] - q[3]
        sp02, sp13 = p[0] + p[2], p[1] + p[3]
        sq02, sq13 = q[0] + q[2], q[1] + q[3]
        a = (sp02 + sp13, dp02 - dq13, sp02 - sp13, dp02 + dq13)
        b = (-(sq02 + sq13), -dq02 - dp13, -(sq02 - sq13), dp13 - dq02)
        for r in range(DFT_RADIX):
            a_ref[r, :, lo:lo + gd] = a[r].astype(BF16)
            b_ref[r, :, lo:lo + gd] = b[r].astype(BF16)


def _fnet_chan(x2d, g, cs, batch, seq):
    t, d = x2d.shape
    qlen = seq // DFT_RADIX
    tm = DFT_CHAN_TILE
    per_q = qlen // tm

    def x_spec(j):
        return pl.BlockSpec((tm, d), lambda b, m: (b * (seq // tm) + j * per_q + m, 0))

    out_spec = pl.BlockSpec((DFT_RADIX, tm, d), lambda b, m: (0, b * per_q + m, 0))
    out_sds = jax.ShapeDtypeStruct((DFT_RADIX, batch * qlen, d), BF16)
    return pl.pallas_call(
        _fnet_chan_kernel,
        grid=(batch, per_q),
        in_specs=[x_spec(0), x_spec(1), x_spec(2), x_spec(3),
                  pl.BlockSpec((1, d), lambda b, m: (0, 0)),
                  pl.BlockSpec(cs.shape, lambda b, m: (0, 0))],
        out_specs=[out_spec, out_spec],
        out_shape=[out_sds, out_sds],
        compiler_params=_params(2),
        name="fnet_chan",
    )(x2d, x2d, x2d, x2d, g, cs)


def _fnet_seq_kernel(c_ref, s_ref, a_ref, b_ref, e_ref, wo_ref, bo_ref, x_ref, o_ref, acc_ref):
    k = pl.program_id(2)

    @pl.when(k == 0)
    def _():
        acc_ref[...] = jnp.zeros_like(acc_ref)

    for r in range(DFT_RADIX):
        acc_ref[r] += (jnp.dot(c_ref[r], a_ref[r], preferred_element_type=F32)
                       + jnp.dot(s_ref[r], b_ref[r], preferred_element_type=F32))

    @pl.when(k == pl.num_programs(2) - 1)
    def _():
        mixed = jnp.dot(e_ref[0], acc_ref[0].astype(BF16), preferred_element_type=F32)
        for r in range(1, DFT_RADIX):
            mixed += jnp.dot(e_ref[r], acc_ref[r].astype(BF16), preferred_element_type=F32)
        o_ref[...] = (jnp.dot(mixed.astype(BF16), wo_ref[...], preferred_element_type=F32)
                      + bo_ref[...] + x_ref[...])


def _fnet_seq(ctab, stab, a, b, wo, bo, x2d, batch, seq):
    d = x2d.shape[1]
    qlen = seq // DFT_RADIX
    tq, tk = DFT_TQ, DFT_TK
    qt, kt = qlen // tq, qlen // tk
    tab_spec = pl.BlockSpec((DFT_RADIX, tq, tk), lambda bb, i, k: (0, i, k))
    in_spec = pl.BlockSpec((DFT_RADIX, tk, d), lambda bb, i, k: (0, bb * kt + k, 0))
    row_spec = pl.BlockSpec((DFT_RADIX * tq, d), lambda bb, i, k: (bb * qt + i, 0))
    out_row = jnp.arange(DFT_RADIX * tq, dtype=I32)[None, :, None]
    src_row = jnp.arange(tq, dtype=I32)[None, None, :]
    res = jnp.arange(DFT_RADIX, dtype=I32)[:, None, None]
    expand = (out_row == DFT_RADIX * src_row + res).astype(BF16)
    return pl.pallas_call(
        _fnet_seq_kernel,
        grid=(batch, qt, kt),
        in_specs=[tab_spec, tab_spec, in_spec, in_spec,
                  pl.BlockSpec(expand.shape, lambda bb, i, k: (0, 0, 0)),
                  pl.BlockSpec((d, d), lambda bb, i, k: (0, 0)),
                  pl.BlockSpec((1, d), lambda bb, i, k: (0, 0)),
                  row_spec],
        out_specs=row_spec,
        out_shape=jax.ShapeDtypeStruct(x2d.shape, F32),
        scratch_shapes=[pltpu.VMEM((DFT_RADIX, tq, d), F32)],
        compiler_params=_params(3),
        name="fnet_seq",
    )(ctab, stab, a, b, expand, wo, bo, x2d)


def _dft_tables(seq):
    qlen = seq // DFT_RADIX
    m = jnp.arange(qlen, dtype=I32)
    ang_a = ((m[:, None] * m[None, :]) % qlen).astype(F32) * F32(2.0 * math.pi / qlen)
    ang_b = (jnp.arange(DFT_RADIX, dtype=I32)[:, None] * m[None, :]).astype(F32) * F32(2.0 * math.pi / seq)
    s_seq = F32(1.0 / math.sqrt(seq))
    cos_a, sin_a = jnp.cos(ang_a)[None], jnp.sin(ang_a)[None]
    cos_b, sin_b = (jnp.cos(ang_b) * s_seq)[:, None, :], (jnp.sin(ang_b) * s_seq)[:, None, :]
    ctab = (cos_a * cos_b - sin_a * sin_b).astype(BF16)
    stab = (sin_a * cos_b + cos_a * sin_b).astype(BF16)
    c = jnp.arange(FNET_GROUP_DIM, dtype=I32)
    cc = (c[:, None] * c[None, :]) % FNET_GROUP_DIM
    angc = cc.astype(F32) * F32(2.0 * math.pi / FNET_GROUP_DIM)
    s_ch = F32(1.0 / math.sqrt(FNET_GROUP_DIM))
    cs = jnp.concatenate([jnp.cos(angc) * s_ch, jnp.sin(angc) * s_ch], axis=1).astype(BF16)
    return ctab, stab, cs


def _fourier_layer(x2d, g, w_o, b_o, batch, seq):
    ctab, stab, cs = _dft_tables(seq)
    a, b = _fnet_chan(x2d, g.reshape(1, -1), cs, batch, seq)
    return _fnet_seq(ctab, stab, a, b, w_o.astype(BF16), b_o.reshape(1, -1), x2d, batch, seq)


def _dup_halves(x, lo_half):
    swapped = pltpu.roll(x, HEAD_DIM, axis=1)
    return jnp.where(lo_half, x, swapped), jnp.where(lo_half, swapped, x)


def _qkv_kernel(x_ref, g_ref, w_ref, b_ref, cos_ref, sin_ref, q_ref, k_ref, v_ref):
    h = _rms(x_ref[...], g_ref[...]).astype(BF16)
    qkv = jnp.dot(h, w_ref[...], preferred_element_type=F32) + b_ref[...]
    cos = cos_ref[...]
    sin = sin_ref[...]
    lane = lax.broadcasted_iota(I32, cos.shape, 1)
    first_half = (lane % HEAD_DIM) < (HEAD_DIM // 2)
    lo_half = lane < HEAD_DIM
    n_q = N_HEADS * HEAD_DIM // LANES
    n_kv = KV_DIM // LANES

    def rope(xc):
        fwd = pltpu.roll(xc, LANES - HEAD_DIM // 2, axis=1)
        bwd = pltpu.roll(xc, HEAD_DIM // 2, axis=1)
        return xc * cos + jnp.where(first_half, fwd, bwd) * sin

    for c in range(n_q):
        r = rope(qkv[:, c * LANES:(c + 1) * LANES])
        q_ref[:, c * LANES:(c + 1) * LANES] = (r * F32(HEAD_DIM ** -0.5)).astype(BF16)
    for c in range(n_kv):
        kc = rope(qkv[:, (n_q + c) * LANES:(n_q + c + 1) * LANES])
        vc = qkv[:, (n_q + n_kv + c) * LANES:(n_q + n_kv + c + 1) * LANES]
        for ref, val in ((k_ref, kc), (v_ref, vc)):
            a, b = _dup_halves(val, lo_half)
            ref[:, (2 * c) * LANES:(2 * c + 1) * LANES] = a.astype(BF16)
            ref[:, (2 * c + 1) * LANES:(2 * c + 2) * LANES] = b.astype(BF16)


def _qkv_rope(x2d, g, w_qkv, b_qkv, seq):
    t, d = x2d.shape
    qd = N_HEADS * HEAD_DIM
    kvd = N_KV_HEADS * LANES
    pos = jnp.arange(seq, dtype=F32)
    inv_freq = ROPE_THETA ** (-jnp.arange(0, HEAD_DIM, 2, dtype=F32) / HEAD_DIM)
    ang = pos[:, None] * inv_freq[None, :]
    cos = jnp.cos(ang)
    sin = jnp.sin(ang)
    reps = LANES // (HEAD_DIM // 2)
    cos_l = jnp.tile(cos, (1, reps))
    sin_l = jnp.tile(jnp.concatenate([-sin, sin], axis=1), (1, LANES // HEAD_DIM))
    spt = seq // ROW_TILE
    return pl.pallas_call(
        _qkv_kernel,
        grid=(t // ROW_TILE,),
        in_specs=[
            pl.BlockSpec((ROW_TILE, d), lambda i: (i, 0)),
            pl.BlockSpec((1, d), lambda i: (0, 0)),
            pl.BlockSpec(w_qkv.shape, lambda i: (0, 0)),
            pl.BlockSpec((1, w_qkv.shape[1]), lambda i: (0, 0)),
            pl.BlockSpec((ROW_TILE, LANES), lambda i: (i % spt, 0)),
            pl.BlockSpec((ROW_TILE, LANES), lambda i: (i % spt, 0)),
        ],
        out_specs=[
            pl.BlockSpec((ROW_TILE, qd), lambda i: (i, 0)),
            pl.BlockSpec((ROW_TILE, kvd), lambda i: (i, 0)),
            pl.BlockSpec((ROW_TILE, kvd), lambda i: (i, 0)),
        ],
        out_shape=[
            jax.ShapeDtypeStruct((t, qd), BF16),
            jax.ShapeDtypeStruct((t, kvd), BF16),
            jax.ShapeDtypeStruct((t, kvd), BF16),
        ],
        compiler_params=_params(1),
        name="qkv_rope",
    )(x2d, g.reshape(1, -1), w_qkv.astype(BF16), b_qkv.reshape(1, -1), cos_l, sin_l)


def _attn_kernel(sink_ref, q_ref, *refs, n_blocks):
    blk = ATTN_BLOCK
    n_kv = ATTN_Q_BLOCKS + 2
    k_blocks, v_blocks = refs[:n_kv], refs[n_kv:2 * n_kv]
    wo_ref, bo_ref, x_ref, o_ref = refs[2 * n_kv:]
    for sub in range(ATTN_Q_BLOCKS):
        jj = pl.program_id(1) * ATTN_Q_BLOCKS + sub
        kcat = jnp.concatenate([r[...] for r in k_blocks[sub:sub + 3]], axis=0)
        vcat = jnp.concatenate([r[...] for r in v_blocks[sub:sub + 3]], axis=0)
        rows = slice(sub * blk, (sub + 1) * blk)
        attn = _attn_block(sink_ref, q_ref, kcat, vcat, jj, rows, n_blocks)
        o_ref[rows, :] = (jnp.dot(attn, wo_ref[...], preferred_element_type=F32)
                          + bo_ref[...] + x_ref[rows, :])


def _attn_block(sink_ref, q_ref, kcat, vcat, j, rows, n_blocks):
    blk = ATTN_BLOCK
    iq = lax.broadcasted_iota(I32, (blk, 3 * blk), 0)
    ik = lax.broadcasted_iota(I32, (blk, 3 * blk), 1)
    kpos = (j - 1) * blk + ik
    mask = (jnp.abs(iq + blk - ik) <= WINDOW) & (kpos >= 0) & (kpos < n_blocks * blk)
    lo_half = lax.broadcasted_iota(I32, (1, LANES), 1) < HEAD_DIM
    ones = jnp.ones((3 * blk, LANES), BF16)
    zero = jnp.zeros((), BF16)
    chunks = []
    for g in range(N_KV_HEADS):
        kd = kcat[:, g * LANES:(g + 1) * LANES]
        vd = vcat[:, g * LANES:(g + 1) * LANES]
        v_half = (jnp.concatenate([jnp.where(lo_half, vd, zero), ones], axis=1),
                  jnp.concatenate([jnp.where(lo_half, zero, vd), ones], axis=1))
        for c in range(g * Q_PER_KV // 2, (g + 1) * Q_PER_KV // 2):
            qc = q_ref[rows, c * LANES:(c + 1) * LANES]
            out = None
            for hi in range(2):
                qh = jnp.where(lo_half, zero, qc) if hi else jnp.where(lo_half, qc, zero)
                s = jnp.where(mask, _nt_dot(qh, kd), NEG_INF)
                sink = sink_ref[2 * c + hi]
                mx = jnp.maximum(jnp.max(s, axis=-1, keepdims=True), sink)
                p = jnp.exp(s - mx).astype(BF16)
                r = jnp.dot(p, v_half[hi], preferred_element_type=F32)
                n = r[:, :LANES] / (r[:, LANES:] + jnp.exp(sink - mx))
                out = n if out is None else out + n
            chunks.append(out.astype(BF16))
    return jnp.concatenate(chunks, axis=1)


def _attention(q, k, v, sink, w_o, b_o, x2d, batch, seq):
    blk = ATTN_BLOCK
    nb = seq // blk
    nq = ATTN_Q_BLOCKS
    qd = q.shape[1]
    kvd = k.shape[1]
    d = x2d.shape[1]

    def kv_spec(off):
        return pl.BlockSpec((blk, kvd), lambda b, j: (b * nb + jnp.clip(nq * j + off, 0, nb - 1), 0))

    kv_specs = [kv_spec(off) for off in range(-1, nq + 1)]
    q_spec = pl.BlockSpec((nq * blk, qd), lambda b, j: (b * (nb // nq) + j, 0))
    x_spec = pl.BlockSpec((nq * blk, d), lambda b, j: (b * (nb // nq) + j, 0))
    return pl.pallas_call(
        functools.partial(_attn_kernel, n_blocks=nb),
        grid=(batch, nb // nq),
        in_specs=[pl.BlockSpec(memory_space=pltpu.SMEM), q_spec] + kv_specs + kv_specs + [
            pl.BlockSpec((qd, d), lambda b, j: (0, 0)),
            pl.BlockSpec((1, d), lambda b, j: (0, 0)),
            x_spec],
        out_specs=x_spec,
        out_shape=jax.ShapeDtypeStruct(x2d.shape, F32),
        compiler_params=_params(2),
        name="band_attn",
    )(sink.astype(F32), q, *([k] * (nq + 2)), *([v] * (nq + 2)), w_o.astype(BF16), b_o.reshape(1, -1), x2d)


def _attention_layer(x2d, g, w_qkv, b_qkv, sink, w_o, b_o, batch, seq):
    q, k, v = _qkv_rope(x2d, g, w_qkv, b_qkv, seq)
    return _attention(q, k, v, sink, w_o, b_o, x2d, batch, seq)


def _router_kernel(x_ref, g_ref, whi_ref, wlo_ref, rb_ref, ut_ref,
                   h_ref, e_ref, gate_ref, rank_ref, cnt_ref, carry_ref):
    i = pl.program_id(0)

    @pl.when(i == 0)
    def _():
        carry_ref[...] = jnp.zeros_like(carry_ref)

    h = _rms(x_ref[...], g_ref[...])
    for ch in range(ROW_CHUNKS):
        h_ref[pl.ds(ch, h.shape[0], stride=ROW_CHUNKS), :] = h[:, ch * LANES:(ch + 1) * LANES]
    h_hi = h.astype(BF16)
    h_lo = (h - h_hi.astype(F32)).astype(BF16)
    logits = (_nt_dot(whi_ref[...], h_hi) + _nt_dot(whi_ref[...], h_lo)
              + _nt_dot(wlo_ref[...], h_hi) + rb_ref[...])
    n_e, tm = logits.shape
    eid = lax.broadcasted_iota(I32, (n_e, tm), 0)
    vals = logits
    top_v, top_i, hots = [], [], []
    for _ in range(TOP_K):
        mx = jnp.max(vals, axis=0, keepdims=True)
        idx = jnp.min(jnp.where(vals == mx, eid, n_e), axis=0, keepdims=True)
        hot = eid == idx
        vals = jnp.where(hot, -jnp.inf, vals)
        top_v.append(mx)
        top_i.append(idx)
        hots.append(hot)
    ex = [jnp.exp(v - top_v[0]) for v in top_v]
    den = ex[0] + ex[1] + ex[2] + ex[3]
    member = (hots[0] | hots[1] | hots[2] | hots[3]).astype(F32)
    cum = jnp.dot(member.astype(BF16), ut_ref[...], preferred_element_type=F32)
    carry = carry_ref[:, 0:1]
    excl = carry + cum - member
    for kk in range(TOP_K):
        e_ref[kk:kk + 1, :] = top_i[kk]
        gate_ref[kk:kk + 1, :] = ex[kk] / den
        rank_ref[kk:kk + 1, :] = jnp.sum(jnp.where(hots[kk], excl, 0.0), axis=0, keepdims=True).astype(I32)
    new_carry = carry + jnp.sum(member, axis=1, keepdims=True)
    carry_ref[...] = jnp.broadcast_to(new_carry, carry_ref.shape)
    cnt_ref[...] = jnp.broadcast_to(new_carry, cnt_ref.shape)


def _router(x2d, g, router_w, router_b):
    t, d = x2d.shape
    tm = ROW_TILE
    wt = router_w.T.astype(F32)
    w_hi = wt.astype(BF16)
    w_lo = (wt - w_hi.astype(F32)).astype(BF16)
    ut = (jnp.arange(tm, dtype=I32)[:, None] <= jnp.arange(tm, dtype=I32)[None, :]).astype(BF16)
    kt_spec = pl.BlockSpec((TOP_K, tm), lambda i: (0, i))
    return pl.pallas_call(
        _router_kernel,
        grid=(t // tm,),
        in_specs=[
            pl.BlockSpec((tm, d), lambda i: (i, 0)),
            pl.BlockSpec((1, d), lambda i: (0, 0)),
            pl.BlockSpec((N_EXPERTS, d), lambda i: (0, 0)),
            pl.BlockSpec((N_EXPERTS, d), lambda i: (0, 0)),
            pl.BlockSpec((N_EXPERTS, 1), lambda i: (0, 0)),
            pl.BlockSpec((tm, tm), lambda i: (0, 0)),
        ],
        out_specs=[
            pl.BlockSpec((tm * ROW_CHUNKS, LANES), lambda i: (i, 0)),
            kt_spec, kt_spec, kt_spec,
            pl.BlockSpec((N_EXPERTS, LANES), lambda i: (0, 0)),
        ],
        out_shape=[
            jax.ShapeDtypeStruct((t * ROW_CHUNKS, LANES), F32),
            jax.ShapeDtypeStruct((TOP_K, t), I32),
            jax.ShapeDtypeStruct((TOP_K, t), F32),
            jax.ShapeDtypeStruct((TOP_K, t), I32),
            jax.ShapeDtypeStruct((N_EXPERTS, LANES), F32),
        ],
        scratch_shapes=[pltpu.VMEM((N_EXPERTS, LANES), F32)],
        compiler_params=_params(1),
        name="router",
    )(x2d, g.reshape(1, -1), w_hi, w_lo, router_b.reshape(-1, 1).astype(F32), ut)


def _expert_kernel(be_ref, base_ref, nv_ref, idx_hbm, h_hbm, wgu_ref, bgu_ref, wd_ref, bd_ref, y_ref,
                   idx0, idx1, xbuf0, xbuf1, wgu_bf, wd_bf, isem, gsem, *, n_blocks):
    i = pl.program_id(0)
    nv = nv_ref[0]
    last_blk = n_blocks - 1
    bm = y_ref.shape[0] // ROW_CHUNKS
    d_ff = wd_bf.shape[0]
    idx_smem = (idx0, idx1)
    xbuf = (xbuf0, xbuf1)
    win = idx0.shape[0]
    nch = ROW_CHUNKS
    c = jnp.maximum(i - 1, 0)

    def idx_copy(blk, s):
        start = pl.multiple_of((base_ref[blk] // LANES) * LANES, LANES)
        return pltpu.make_async_copy(idx_hbm.at[pl.ds(start, win)], idx_smem[s], isem.at[s])

    def rows_done(s):
        return pltpu.make_async_copy(h_hbm.at[pl.ds(0, bm * nch), :], xbuf[s], gsem.at[s])

    @pl.when(i == 0)
    def _():
        idx_copy(0, 0).start()
        rows_done(1).start()

    @pl.when((i <= 1) | (be_ref[c] != be_ref[jnp.maximum(c - 1, 0)]))
    def _():
        wgu_bf[...] = wgu_ref[0, 0].astype(BF16)
        wd_bf[...] = wd_ref[0, 0].astype(BF16)

    def step(s):
        blk = jnp.minimum(i, last_blk)
        idx_copy(blk, s).wait()
        rows_done(1 - s).wait()
        idx_copy(jnp.minimum(i + 1, last_blk), 1 - s).start()
        off = lax.rem(base_ref[blk], LANES)
        for r in range(bm):
            tok = idx_smem[s][off + r]
            src = h_hbm.at[pl.ds(pl.multiple_of(tok * nch, nch), nch), :]
            pltpu.make_async_copy(src, xbuf[s].at[pl.ds(r * nch, nch), :], gsem.at[s]).start(priority=r % 2)
        x = jnp.concatenate([xbuf[1 - s][pl.ds(ch, bm, stride=nch), :] for ch in range(nch)], axis=1)
        gu = jnp.dot(x.astype(BF16), wgu_bf[...], preferred_element_type=F32) + bgu_ref[0, 0]
        gate = jnp.minimum(gu[:, :d_ff], SWIGLU_LIMIT)
        up = jnp.clip(gu[:, d_ff:], -SWIGLU_LIMIT, SWIGLU_LIMIT)
        glu = gate * jax.nn.sigmoid(gate * SWIGLU_ALPHA)
        act = ((up + 1.0) * glu).astype(BF16)
        y = jnp.dot(act, wd_bf[...], preferred_element_type=F32) + bd_ref[0, 0]
        for ch in range(nch):
            y_ref[pl.ds(ch, bm, stride=nch), :] = y[:, ch * LANES:(ch + 1) * LANES]

    def drain(s):
        idx_copy(last_blk, 1 - s).wait()
        rows_done(s).wait()

    for s in range(2):
        pl.when((lax.rem(i, 2) == s) & (i <= nv))(functools.partial(step, s))
    for s in range(2):
        pl.when((lax.rem(i, 2) == s) & (i == nv))(functools.partial(drain, s))

    @pl.when(i > nv)
    def _():
        y_ref[...] = jnp.zeros_like(y_ref)


def _experts(h8, sorted_tok, block_expert, block_base, n_valid, layer, w_gu, b_gu, w_down, b_down, n_blocks):
    bm = MOE_BM
    depth, n_e, d_ff, d = w_down.shape

    def cidx(i):
        return jnp.maximum(i - 1, 0)

    def wmap(i, be, base, nv):
        return (layer, be[cidx(i)], 0, 0)

    grid_spec = pltpu.PrefetchScalarGridSpec(
        num_scalar_prefetch=3,
        grid=(n_blocks + 1,),
        in_specs=[
            pl.BlockSpec(memory_space=pl.ANY),
            pl.BlockSpec(memory_space=pl.ANY),
            pl.BlockSpec((1, 1, d, 2 * d_ff), wmap),
            pl.BlockSpec((1, 1, 1, 2 * d_ff), wmap),
            pl.BlockSpec((1, 1, d_ff, d), wmap),
            pl.BlockSpec((1, 1, 1, d), wmap),
        ],
        out_specs=pl.BlockSpec((bm * ROW_CHUNKS, LANES), lambda i, be, base, nv: (cidx(i), 0)),
        scratch_shapes=[
            pltpu.SMEM((bm + LANES,), I32),
            pltpu.SMEM((bm + LANES,), I32),
            pltpu.VMEM((bm * ROW_CHUNKS, LANES), F32),
            pltpu.VMEM((bm * ROW_CHUNKS, LANES), F32),
            pltpu.VMEM((d, 2 * d_ff), BF16),
            pltpu.VMEM((d_ff, d), BF16),
            pltpu.SemaphoreType.DMA((2,)),
            pltpu.SemaphoreType.DMA((2,)),
        ],
    )
    return pl.pallas_call(
        functools.partial(_expert_kernel, n_blocks=n_blocks),
        grid_spec=grid_spec,
        out_shape=jax.ShapeDtypeStruct((n_blocks * bm * ROW_CHUNKS, LANES), F32),
        compiler_params=_params(1),
        name="moe_experts",
    )(block_expert, block_base, n_valid, sorted_tok, h8, w_gu, b_gu.reshape(depth, n_e, 1, -1), w_down,
      b_down.reshape(depth, n_e, 1, -1))


def _combine_kernel(didx_hbm, ys_hbm, x_ref, gt_ref, fg_ref, o_ref, idx0, idx1, buf0, buf1, isem, gsem,
                    *, n_tiles, final_norm):
    i = pl.program_id(0)
    last = n_tiles - 1
    idx_smem = (idx0, idx1)
    buf = (buf0, buf1)
    rows = idx0.shape[0]
    tc = rows // TOP_K
    nch = ROW_CHUNKS

    def idx_copy(blk, s):
        return pltpu.make_async_copy(didx_hbm.at[pl.ds(blk * rows, rows)], idx_smem[s], isem.at[s])

    def rows_done(s):
        return pltpu.make_async_copy(ys_hbm.at[pl.ds(0, rows * nch), :], buf[s], gsem.at[s])

    @pl.when(i == 0)
    def _():
        idx_copy(0, 0).start()
        rows_done(1).start()

    def step(s):
        idx_copy(jnp.minimum(i, last), s).wait()
        rows_done(1 - s).wait()
        idx_copy(jnp.minimum(i + 1, last), 1 - s).start()
        for r in range(rows):
            src = ys_hbm.at[pl.ds(pl.multiple_of(idx_smem[s][r] * nch, nch), nch), :]
            pltpu.make_async_copy(src, buf[s].at[pl.ds(r * nch, nch), :], gsem.at[s]).start(priority=r % 2)
        acc = x_ref[...]
        gt = gt_ref[...]
        for kk in range(TOP_K):
            yk = jnp.concatenate([buf[1 - s][pl.ds(kk * tc * nch + ch, tc, stride=nch), :]
                                  for ch in range(nch)], axis=1)
            acc = acc + gt[:, kk:kk + 1] * yk
        if final_norm:
            acc = _rms(acc, fg_ref[...])
        o_ref[...] = acc

    for s in range(2):
        pl.when(lax.rem(i, 2) == s)(functools.partial(step, s))

    @pl.when(i == n_tiles)
    def _():
        s_last = n_tiles % 2
        idx_copy(last, 1 - s_last).wait()
        rows_done(s_last).wait()


def _combine(ys8, dest_tiles, gates_t, x2d, final_g):
    t, d = x2d.shape
    tc = COMBINE_TILE
    n = t // tc
    rows = TOP_K * tc

    def cidx(i):
        return jnp.maximum(i - 1, 0)

    fg = jnp.ones((1, d), F32) if final_g is None else final_g.reshape(1, -1).astype(F32)
    return pl.pallas_call(
        functools.partial(_combine_kernel, n_tiles=n, final_norm=final_g is not None),
        grid=(n + 1,),
        in_specs=[
            pl.BlockSpec(memory_space=pl.ANY),
            pl.BlockSpec(memory_space=pl.ANY),
            pl.BlockSpec((tc, d), lambda i: (cidx(i), 0)),
            pl.BlockSpec((tc, TOP_K), lambda i: (cidx(i), 0)),
            pl.BlockSpec((1, d), lambda i: (0, 0)),
        ],
        out_specs=pl.BlockSpec((tc, d), lambda i: (cidx(i), 0)),
        out_shape=jax.ShapeDtypeStruct((t, d), F32),
        scratch_shapes=[
            pltpu.SMEM((rows,), I32),
            pltpu.SMEM((rows,), I32),
            pltpu.VMEM((rows * ROW_CHUNKS, LANES), F32),
            pltpu.VMEM((rows * ROW_CHUNKS, LANES), F32),
            pltpu.SemaphoreType.DMA((2,)),
            pltpu.SemaphoreType.DMA((2,)),
        ],
        compiler_params=_params(1),
        name="moe_combine",
    )(dest_tiles, ys8, x2d, gates_t, fg)


def _moe_layer(x2d, g, router_w, router_b, layer, w_gu, b_gu, w_down, b_down, final_g):
    t, d = x2d.shape
    bm = MOE_BM
    a = t * TOP_K
    n_blocks = a // bm + N_EXPERTS

    h, e_kt, gate_kt, rank_kt, cnt = _router(x2d, g, router_w, router_b)

    counts = cnt[:, 0].astype(I32)
    blocks_e = (counts + bm - 1) // bm
    blk_end = jnp.cumsum(blocks_e)
    blk_start = blk_end - blocks_e
    n_valid = blk_end[-1]
    tok_start = jnp.cumsum(counts) - counts

    a_id = jnp.arange(t, dtype=I32)[None, :] * TOP_K + jnp.arange(TOP_K, dtype=I32)[:, None]
    sorted_key = jnp.sort((e_kt * a + a_id).reshape(-1))
    sorted_tok = jnp.pad((sorted_key % a) // TOP_K, (0, bm + LANES))

    eids = jnp.arange(N_EXPERTS, dtype=I32)
    bid = jnp.arange(n_blocks, dtype=I32)
    be = jnp.minimum(jnp.sum((bid[:, None] >= blk_end[None, :]).astype(I32), axis=1), N_EXPERTS - 1)
    hot = be[:, None] == eids[None, :]
    base = (jnp.sum(jnp.where(hot, tok_start[None, :], 0), axis=1)
            + (bid - jnp.sum(jnp.where(hot, blk_start[None, :], 0), axis=1)) * bm)
    valid = bid < n_valid
    last_e = jnp.sum(jnp.where(bid == n_valid - 1, be, 0))
    block_expert = jnp.where(valid, be, last_e).astype(I32)
    block_base = jnp.where(valid, base, 0).astype(I32)

    ys = _experts(h, sorted_tok, block_expert, block_base, n_valid.reshape(1).astype(I32), layer,
                  w_gu, b_gu, w_down, b_down, n_blocks)

    dest = rank_kt
    for e in range(N_EXPERTS):
        dest = dest + jnp.where(e_kt == e, blk_start[e] * bm, 0)
    tc = COMBINE_TILE
    dest_tiles = dest.reshape(TOP_K, t // tc, tc).transpose(1, 0, 2).reshape(-1).astype(I32)
    return _combine(ys, dest_tiles, gate_kt.T, x2d, final_g)


def kernel(x, norm_mix_g, norm_ffn_g, fnet_w_o, fnet_b_o, attn_w_qkv, attn_b_qkv, attn_sink, attn_w_o, attn_b_o, router_w, router_b, expert_w_gu, expert_b_gu, expert_w_down, expert_b_down, final_norm_g):
    batch, seq, d = x.shape
    depth = norm_mix_g.shape[0]
    x2d = x.reshape(batch * seq, d)
    for i in range(depth):
        j = i // 2
        if i % 2 == 0:
            x2d = _fourier_layer(x2d, norm_mix_g[i], fnet_w_o[j], fnet_b_o[j], batch, seq)
        else:
            x2d = _attention_layer(x2d, norm_mix_g[i], attn_w_qkv[j], attn_b_qkv[j], attn_sink[j],
                                   attn_w_o[j], attn_b_o[j], batch, seq)
        fg = final_norm_g if i == depth - 1 else None
        x2d = _moe_layer(x2d, norm_ffn_g[i], router_w[i], router_b[i], i, expert_w_gu, expert_b_gu,
                         expert_w_down, expert_b_down, fg)
    return x2d.reshape(batch, seq, d)
```

```python
import functools
import math

import jax
import jax.numpy as jnp
from jax import lax
from jax.experimental import pallas as pl
from jax.experimental.pallas import tpu as pltpu

F32 = jnp.float32
BF16 = jnp.bfloat16
I32 = jnp.int32

D_MODEL = 1024
FNET_GROUPS = 8
FNET_GROUP_DIM = D_MODEL // FNET_GROUPS
HEAD_DIM = 64
N_HEADS = 16
N_KV_HEADS = 4
Q_PER_KV = N_HEADS // N_KV_HEADS
KV_DIM = N_KV_HEADS * HEAD_DIM
WINDOW = 128
ATTN_BLOCK = 128
ROPE_THETA = 10000.0
N_EXPERTS = 32
TOP_K = 4
D_FF = D_MODEL
SWIGLU_ALPHA = 1.702
SWIGLU_LIMIT = 7.0
RMS_EPS = 1e-5
NEG_INF = -1e30

LANES = 128
ROW_CHUNKS = D_MODEL // LANES
VMEM_LIMIT = 56 * 1024 * 1024

ROW_TILE = 1024
MOE_BM = 512
COMBINE_TILE = 128
ATTN_Q_BLOCKS = 4
DFT_CHAN_TILE = 256
DFT_TQ = 256
DFT_TK = 512

_ARB = pltpu.ARBITRARY


def _params(n_axes):
    return pltpu.CompilerParams(dimension_semantics=(_ARB,) * n_axes, vmem_limit_bytes=VMEM_LIMIT)


def _nt_dot(a, b):
    return lax.dot_general(a, b, (((1,), (1,)), ((), ())), preferred_element_type=F32)


def _rms(x, g):
    ms = jnp.mean(x * x, axis=-1, keepdims=True)
    return (x * lax.rsqrt(ms + RMS_EPS)) * g


DFT_RADIX = 4


def _fnet_chan_kernel(x0_ref, x1_ref, x2_ref, x3_ref, g_ref, cs_ref, a_ref, b_ref):
    hs = [_rms(xr[...], g_ref[...]).astype(BF16) for xr in (x0_ref, x1_ref, x2_ref, x3_ref)]
    gd = FNET_GROUP_DIM
    for gi in range(FNET_GROUPS):
        lo = gi * gd
        pq = [jnp.dot(h[:, lo:lo + gd], cs_ref[...], preferred_element_type=F32) for h in hs]
        p = [v[:, :gd] for v in pq]
        q = [v[:, gd:] for v in pq]
        dp02, dp13 = p[0] - p[2], p[1] - p[3]
        dq02, dq13 = q[0] - q[2], q[1] - q[3]
        sp02, sp13 = p[0] + p[2], p[1] + p[3]
        sq02, sq13 = q[0] + q[2], q[1] + q[3]
        a = (sp02 + sp13, dp02 - dq13, sp02 - sp13, dp02 + dq13)
        b = (-(sq02 + sq13), -dq02 - dp13, -(sq02 - sq13), dp13 - dq02)
        for r in range(DFT_RADIX):
            a_ref[r, :, lo:lo + gd] = a[r].astype(BF16)
            b_ref[r, :, lo:lo + gd] = b[r].astype(BF16)


def _fnet_chan(x2d, g, cs, batch, seq):
    t, d = x2d.shape
    qlen = seq // DFT_RADIX
    tm = DFT_CHAN_TILE
    per_q = qlen // tm

    def x_spec(j):
        return pl.BlockSpec((tm, d), lambda b, m: (b * (seq // tm) + j * per_q + m, 0))

    out_spec = pl.BlockSpec((DFT_RADIX, tm, d), lambda b, m: (0, b * per_q + m, 0))
    out_sds = jax.ShapeDtypeStruct((DFT_RADIX, batch * qlen, d), BF16)
    return pl.pallas_call(
        _fnet_chan_kernel,
        grid=(batch, per_q),
        in_specs=[x_spec(0), x_spec(1), x_spec(2), x_spec(3),
                  pl.BlockSpec((1, d), lambda b, m: (0, 0)),
                  pl.BlockSpec(cs.shape, lambda b, m: (0, 0))],
        out_specs=[out_spec, out_spec],
        out_shape=[out_sds, out_sds],
        compiler_params=_params(2),
        name="fnet_chan",
    )(x2d, x2d, x2d, x2d, g, cs)


def _fnet_seq_kernel(c_ref, s_ref, a_ref, b_ref, e_ref, wo_ref, bo_ref, x_ref, o_ref, acc_ref):
    k = pl.program_id(2)

    @pl.when(k == 0)
    def _():
        acc_ref[...] = jnp.zeros_like(acc_ref)

    for r in range(DFT_RADIX):
        acc_ref[r] += (jnp.dot(c_ref[r], a_ref[r], preferred_element_type=F32)
                       + jnp.dot(s_ref[r], b_ref[r], preferred_element_type=F32))

    @pl.when(k == pl.num_programs(2) - 1)
    def _():
        mixed = jnp.dot(e_ref[0], acc_ref[0].astype(BF16), preferred_element_type=F32)
        for r in range(1, DFT_RADIX):
            mixed += jnp.dot(e_ref[r], acc_ref[r].astype(BF16), preferred_element_type=F32)
        o_ref[...] = (jnp.dot(mixed.astype(BF16), wo_ref[...], preferred_element_type=F32)
                      + bo_ref[...] + x_ref[...])


def _fnet_seq(ctab, stab, a, b, wo, bo, x2d, batch, seq):
    d = x2d.shape[1]
    qlen = seq // DFT_RADIX
    tq, tk = DFT_TQ, DFT_TK
    qt, kt = qlen // tq, qlen // tk
    tab_spec = pl.BlockSpec((DFT_RADIX, tq, tk), lambda bb, i, k: (0, i, k))
    in_spec = pl.BlockSpec((DFT_RADIX, tk, d), lambda bb, i, k: (0, bb * kt + k, 0))
    row_spec = pl.BlockSpec((DFT_RADIX * tq, d), lambda bb, i, k: (bb * qt + i, 0))
    out_row = jnp.arange(DFT_RADIX * tq, dtype=I32)[None, :, None]
    src_row = jnp.arange(tq, dtype=I32)[None, None, :]
    res = jnp.arange(DFT_RADIX, dtype=I32)[:, None, None]
    expand = (out_row == DFT_RADIX * src_row + res).astype(BF16)
    return pl.pallas_call(
        _fnet_seq_kernel,
        grid=(batch, qt, kt),
        in_specs=[tab_spec, tab_spec, in_spec, in_spec,
                  pl.BlockSpec(expand.shape, lambda bb, i, k: (0, 0, 0)),
                  pl.BlockSpec((d, d), lambda bb, i, k: (0, 0)),
                  pl.BlockSpec((1, d), lambda bb, i, k: (0, 0)),
                  row_spec],
        out_specs=row_spec,
        out_shape=jax.ShapeDtypeStruct(x2d.shape, F32),
        scratch_shapes=[pltpu.VMEM((DFT_RADIX, tq, d), F32)],
        compiler_params=_params(3),
        name="fnet_seq",
    )(ctab, stab, a, b, expand, wo, bo, x2d)


def _dft_tables(seq):
    qlen = seq // DFT_RADIX
    m = jnp.arange(qlen, dtype=I32)
    ang_a = ((m[:, None] * m[None, :]) % qlen).astype(F32) * F32(2.0 * math.pi / qlen)
    ang_b = (jnp.arange(DFT_RADIX, dtype=I32)[:, None] * m[None, :]).astype(F32) * F32(2.0 * math.pi / seq)
    s_seq = F32(1.0 / math.sqrt(seq))
    cos_a, sin_a = jnp.cos(ang_a)[None], jnp.sin(ang_a)[None]
    cos_b, sin_b = (jnp.cos(ang_b) * s_seq)[:, None, :], (jnp.sin(ang_b) * s_seq)[:, None, :]
    ctab = (cos_a * cos_b - sin_a * sin_b).astype(BF16)
    stab = (sin_a * cos_b + cos_a * sin_b).astype(BF16)
    c = jnp.arange(FNET_GROUP_DIM, dtype=I32)
    cc = (c[:, None] * c[None, :]) % FNET_GROUP_DIM
    angc = cc.astype(F32) * F32(2.0 * math.pi / FNET_GROUP_DIM)
    s_ch = F32(1.0 / math.sqrt(FNET_GROUP_DIM))
    cs = jnp.concatenate([jnp.cos(angc) * s_ch, jnp.sin(angc) * s_ch], axis=1).astype(BF16)
    return ctab, stab, cs


def _fourier_layer(x2d, g, w_o, b_o, batch, seq):
    ctab, stab, cs = _dft_tables(seq)
    a, b = _fnet_chan(x2d, g.reshape(1, -1), cs, batch, seq)
    return _fnet_seq(ctab, stab, a, b, w_o.astype(BF16), b_o.reshape(1, -1), x2d, batch, seq)


def _dup_halves(x, lo_half):
    swapped = pltpu.roll(x, HEAD_DIM, axis=1)
    return jnp.where(lo_half, x, swapped), jnp.where(lo_half, swapped, x)


def _qkv_kernel(x_ref, g_ref, w_ref, b_ref, cos_ref, sin_ref, q_ref, k_ref, v_ref):
    _qkv_store(x_ref[...], g_ref, w_ref, b_ref, cos_ref, sin_ref, q_ref, k_ref, v_ref)


def _qkv_store(x, g_ref, w_ref, b_ref, cos_ref, sin_ref, q_ref, k_ref, v_ref):
    h = _rms(x, g_ref[...]).astype(BF16)
    qkv = jnp.dot(h, w_ref[...], preferred_element_type=F32) + b_ref[...]
    cos = cos_ref[...]
    sin = sin_ref[...]
    lane = lax.broadcasted_iota(I32, cos.shape, 1)
    first_half = (lane % HEAD_DIM) < (HEAD_DIM // 2)
    lo_half = lane < HEAD_DIM
    n_q = N_HEADS * HEAD_DIM // LANES
    n_kv = KV_DIM // LANES

    def rope(xc):
        fwd = pltpu.roll(xc, LANES - HEAD_DIM // 2, axis=1)
        bwd = pltpu.roll(xc, HEAD_DIM // 2, axis=1)
        return xc * cos + jnp.where(first_half, fwd, bwd) * sin

    for c in range(n_q):
        r = rope(qkv[:, c * LANES:(c + 1) * LANES])
        q_ref[:, c * LANES:(c + 1) * LANES] = (r * F32(HEAD_DIM ** -0.5)).astype(BF16)
    for c in range(n_kv):
        kc = rope(qkv[:, (n_q + c) * LANES:(n_q + c + 1) * LANES])
        vc = qkv[:, (n_q + n_kv + c) * LANES:(n_q + n_kv + c + 1) * LANES]
        for ref, val in ((k_ref, kc), (v_ref, vc)):
            a, b = _dup_halves(val, lo_half)
            ref[:, (2 * c) * LANES:(2 * c + 1) * LANES] = a.astype(BF16)
            ref[:, (2 * c + 1) * LANES:(2 * c + 2) * LANES] = b.astype(BF16)


def _rope_tables(seq):
    pos = jnp.arange(seq, dtype=F32)
    inv_freq = ROPE_THETA ** (-jnp.arange(0, HEAD_DIM, 2, dtype=F32) / HEAD_DIM)
    ang = pos[:, None] * inv_freq[None, :]
    cos = jnp.cos(ang)
    sin = jnp.sin(ang)
    reps = LANES // (HEAD_DIM // 2)
    cos_l = jnp.tile(cos, (1, reps))
    sin_l = jnp.tile(jnp.concatenate([-sin, sin], axis=1), (1, LANES // HEAD_DIM))
    return cos_l, sin_l


def _qkv_rope(x2d, g, w_qkv, b_qkv, seq):
    t, d = x2d.shape
    qd = N_HEADS * HEAD_DIM
    kvd = N_KV_HEADS * LANES
    cos_l, sin_l = _rope_tables(seq)
    spt = seq // ROW_TILE
    return pl.pallas_call(
        _qkv_kernel,
        grid=(t // ROW_TILE,),
        in_specs=[
            pl.BlockSpec((ROW_TILE, d), lambda i: (i, 0)),
            pl.BlockSpec((1, d), lambda i: (0, 0)),
            pl.BlockSpec(w_qkv.shape, lambda i: (0, 0)),
            pl.BlockSpec((1, w_qkv.shape[1]), lambda i: (0, 0)),
            pl.BlockSpec((ROW_TILE, LANES), lambda i: (i % spt, 0)),
            pl.BlockSpec((ROW_TILE, LANES), lambda i: (i % spt, 0)),
        ],
        out_specs=[
            pl.BlockSpec((ROW_TILE, qd), lambda i: (i, 0)),
            pl.BlockSpec((ROW_TILE, kvd), lambda i: (i, 0)),
            pl.BlockSpec((ROW_TILE, kvd), lambda i: (i, 0)),
        ],
        out_shape=[
            jax.ShapeDtypeStruct((t, qd), BF16),
            jax.ShapeDtypeStruct((t, kvd), BF16),
            jax.ShapeDtypeStruct((t, kvd), BF16),
        ],
        compiler_params=_params(1),
        name="qkv_rope",
    )(x2d, g.reshape(1, -1), w_qkv.astype(BF16), b_qkv.reshape(1, -1), cos_l, sin_l)


def _attn_kernel(sink_ref, q_ref, *refs, n_blocks):
    blk = ATTN_BLOCK
    n_kv = ATTN_Q_BLOCKS + 2
    k_blocks, v_blocks = refs[:n_kv], refs[n_kv:2 * n_kv]
    wo_ref, bo_ref, x_ref, o_ref = refs[2 * n_kv:]
    for sub in range(ATTN_Q_BLOCKS):
        jj = pl.program_id(1) * ATTN_Q_BLOCKS + sub
        kcat = jnp.concatenate([r[...] for r in k_blocks[sub:sub + 3]], axis=0)
        vcat = jnp.concatenate([r[...] for r in v_blocks[sub:sub + 3]], axis=0)
        rows = slice(sub * blk, (sub + 1) * blk)
        attn = _attn_block(sink_ref, q_ref, kcat, vcat, jj, rows, n_blocks)
        o_ref[rows, :] = (jnp.dot(attn, wo_ref[...], preferred_element_type=F32)
                          + bo_ref[...] + x_ref[rows, :])


def _attn_block(sink_ref, q_ref, kcat, vcat, j, rows, n_blocks):
    blk = ATTN_BLOCK
    iq = lax.broadcasted_iota(I32, (blk, 3 * blk), 0)
    ik = lax.broadcasted_iota(I32, (blk, 3 * blk), 1)
    kpos = (j - 1) * blk + ik
    mask = (jnp.abs(iq + blk - ik) <= WINDOW) & (kpos >= 0) & (kpos < n_blocks * blk)
    lo_half = lax.broadcasted_iota(I32, (1, LANES), 1) < HEAD_DIM
    ones = jnp.ones((3 * blk, LANES), BF16)
    zero = jnp.zeros((), BF16)
    chunks = []
    for g in range(N_KV_HEADS):
        kd = kcat[:, g * LANES:(g + 1) * LANES]
        vd = vcat[:, g * LANES:(g + 1) * LANES]
        v_half = (jnp.concatenate([jnp.where(lo_half, vd, zero), ones], axis=1),
                  jnp.concatenate([jnp.where(lo_half, zero, vd), ones], axis=1))
        for c in range(g * Q_PER_KV // 2, (g + 1) * Q_PER_KV // 2):
            qc = q_ref[rows, c * LANES:(c + 1) * LANES]
            out = None
            for hi in range(2):
                qh = jnp.where(lo_half, zero, qc) if hi else jnp.where(lo_half, qc, zero)
                s = jnp.where(mask, _nt_dot(qh, kd), NEG_INF)
                sink = sink_ref[2 * c + hi]
                mx = jnp.maximum(jnp.max(s, axis=-1, keepdims=True), sink)
                p = jnp.exp(s - mx).astype(BF16)
                r = jnp.dot(p, v_half[hi], preferred_element_type=F32)
                n = r[:, :LANES] / (r[:, LANES:] + jnp.exp(sink - mx))
                out = n if out is None else out + n
            chunks.append(out.astype(BF16))
    return jnp.concatenate(chunks, axis=1)


def _attention(q, k, v, sink, w_o, b_o, x2d, batch, seq):
    blk = ATTN_BLOCK
    nb = seq // blk
    nq = ATTN_Q_BLOCKS
    qd = q.shape[1]
    kvd = k.shape[1]
    d = x2d.shape[1]

    def kv_spec(off):
        return pl.BlockSpec((blk, kvd), lambda b, j: (b * nb + jnp.clip(nq * j + off, 0, nb - 1), 0))

    kv_specs = [kv_spec(off) for off in range(-1, nq + 1)]
    q_spec = pl.BlockSpec((nq * blk, qd), lambda b, j: (b * (nb // nq) + j, 0))
    x_spec = pl.BlockSpec((nq * blk, d), lambda b, j: (b * (nb // nq) + j, 0))
    return pl.pallas_call(
        functools.partial(_attn_kernel, n_blocks=nb),
        grid=(batch, nb // nq),
        in_specs=[pl.BlockSpec(memory_space=pltpu.SMEM), q_spec] + kv_specs + kv_specs + [
            pl.BlockSpec((qd, d), lambda b, j: (0, 0)),
            pl.BlockSpec((1, d), lambda b, j: (0, 0)),
            x_spec],
        out_specs=x_spec,
        out_shape=jax.ShapeDtypeStruct(x2d.shape, F32),
        compiler_params=_params(2),
        name="band_attn",
    )(sink.astype(F32), q, *([k] * (nq + 2)), *([v] * (nq + 2)), w_o.astype(BF16), b_o.reshape(1, -1), x2d)


def _attention_layer(x2d, g, w_qkv, b_qkv, sink, w_o, b_o, batch, seq, qkv=None):
    q, k, v = _qkv_rope(x2d, g, w_qkv, b_qkv, seq) if qkv is None else qkv
    return _attention(q, k, v, sink, w_o, b_o, x2d, batch, seq)


def _router_kernel(x_ref, g_ref, whi_ref, wlo_ref, rb_ref, ut_ref,
                   h_ref, e_ref, gate_ref, rank_ref, cnt_ref, carry_ref):
    i = pl.program_id(0)

    @pl.when(i == 0)
    def _():
        carry_ref[...] = jnp.zeros_like(carry_ref)

    h = _rms(x_ref[...], g_ref[...])
    for ch in range(ROW_CHUNKS):
        h_ref[pl.ds(ch, h.shape[0], stride=ROW_CHUNKS), :] = h[:, ch * LANES:(ch + 1) * LANES]
    h_hi = h.astype(BF16)
    h_lo = (h - h_hi.astype(F32)).astype(BF16)
    logits = (_nt_dot(whi_ref[...], h_hi) + _nt_dot(whi_ref[...], h_lo)
              + _nt_dot(wlo_ref[...], h_hi) + rb_ref[...])
    n_e, tm = logits.shape
    eid = lax.broadcasted_iota(I32, (n_e, tm), 0)
    vals = logits
    top_v, top_i, hots = [], [], []
    for _ in range(TOP_K):
        mx = jnp.max(vals, axis=0, keepdims=True)
        idx = jnp.min(jnp.where(vals == mx, eid, n_e), axis=0, keepdims=True)
        hot = eid == idx
        vals = jnp.where(hot, -jnp.inf, vals)
        top_v.append(mx)
        top_i.append(idx)
        hots.append(hot)
    ex = [jnp.exp(v - top_v[0]) for v in top_v]
    den = ex[0] + ex[1] + ex[2] + ex[3]
    member = (hots[0] | hots[1] | hots[2] | hots[3]).astype(F32)
    cum = jnp.dot(member.astype(BF16), ut_ref[...], preferred_element_type=F32)
    carry = carry_ref[:, 0:1]
    excl = carry + cum - member
    for kk in range(TOP_K):
        e_ref[kk:kk + 1, :] = top_i[kk]
        gate_ref[kk:kk + 1, :] = ex[kk] / den
        rank_ref[kk:kk + 1, :] = jnp.sum(jnp.where(hots[kk], excl, 0.0), axis=0, keepdims=True).astype(I32)
    new_carry = carry + jnp.sum(member, axis=1, keepdims=True)
    carry_ref[...] = jnp.broadcast_to(new_carry, carry_ref.shape)
    cnt_ref[...] = jnp.broadcast_to(new_carry, cnt_ref.shape)


def _router(x2d, g, router_w, router_b):
    t, d = x2d.shape
    tm = ROW_TILE
    wt = router_w.T.astype(F32)
    w_hi = wt.astype(BF16)
    w_lo = (wt - w_hi.astype(F32)).astype(BF16)
    ut = (jnp.arange(tm, dtype=I32)[:, None] <= jnp.arange(tm, dtype=I32)[None, :]).astype(BF16)
    kt_spec = pl.BlockSpec((TOP_K, tm), lambda i: (0, i))
    return pl.pallas_call(
        _router_kernel,
        grid=(t // tm,),
        in_specs=[
            pl.BlockSpec((tm, d), lambda i: (i, 0)),
            pl.BlockSpec((1, d), lambda i: (0, 0)),
            pl.BlockSpec((N_EXPERTS, d), lambda i: (0, 0)),
            pl.BlockSpec((N_EXPERTS, d), lambda i: (0, 0)),
            pl.BlockSpec((N_EXPERTS, 1), lambda i: (0, 0)),
            pl.BlockSpec((tm, tm), lambda i: (0, 0)),
        ],
        out_specs=[
            pl.BlockSpec((tm * ROW_CHUNKS, LANES), lambda i: (i, 0)),
            kt_spec, kt_spec, kt_spec,
            pl.BlockSpec((N_EXPERTS, LANES), lambda i: (0, 0)),
        ],
        out_shape=[
            jax.ShapeDtypeStruct((t * ROW_CHUNKS, LANES), F32),
            jax.ShapeDtypeStruct((TOP_K, t), I32),
            jax.ShapeDtypeStruct((TOP_K, t), F32),
            jax.ShapeDtypeStruct((TOP_K, t), I32),
            jax.ShapeDtypeStruct((N_EXPERTS, LANES), F32),
        ],
        scratch_shapes=[pltpu.VMEM((N_EXPERTS, LANES), F32)],
        compiler_params=_params(1),
        name="router",
    )(x2d, g.reshape(1, -1), w_hi, w_lo, router_b.reshape(-1, 1).astype(F32), ut)


def _expert_kernel(be_ref, base_ref, nv_ref, idx_hbm, h_hbm, wgu_ref, bgu_ref, wd_ref, bd_ref, y_ref,
                   idx0, idx1, xbuf0, xbuf1, wgu_bf, wd_bf, isem, gsem, *, n_blocks):
    i = pl.program_id(0)
    nv = nv_ref[0]
    last_blk = n_blocks - 1
    bm = y_ref.shape[0] // ROW_CHUNKS
    d_ff = wd_bf.shape[0]
    idx_smem = (idx0, idx1)
    xbuf = (xbuf0, xbuf1)
    win = idx0.shape[0]
    nch = ROW_CHUNKS
    c = jnp.maximum(i - 1, 0)

    def idx_copy(blk, s):
        start = pl.multiple_of((base_ref[blk] // LANES) * LANES, LANES)
        return pltpu.make_async_copy(idx_hbm.at[pl.ds(start, win)], idx_smem[s], isem.at[s])

    def rows_done(s):
        return pltpu.make_async_copy(h_hbm.at[pl.ds(0, bm * nch), :], xbuf[s], gsem.at[s])

    @pl.when(i == 0)
    def _():
        idx_copy(0, 0).start()
        rows_done(1).start()

    @pl.when((i <= 1) | (be_ref[c] != be_ref[jnp.maximum(c - 1, 0)]))
    def _():
        wgu_bf[...] = wgu_ref[0, 0].astype(BF16)
        wd_bf[...] = wd_ref[0, 0].astype(BF16)

    def step(s):
        blk = jnp.minimum(i, last_blk)
        idx_copy(blk, s).wait()
        rows_done(1 - s).wait()
        idx_copy(jnp.minimum(i + 1, last_blk), 1 - s).start()
        off = lax.rem(base_ref[blk], LANES)
        for r in range(bm):
            tok = idx_smem[s][off + r]
            src = h_hbm.at[pl.ds(pl.multiple_of(tok * nch, nch), nch), :]
            pltpu.make_async_copy(src, xbuf[s].at[pl.ds(r * nch, nch), :], gsem.at[s]).start(priority=r % 2)
        x = jnp.concatenate([xbuf[1 - s][pl.ds(ch, bm, stride=nch), :] for ch in range(nch)], axis=1)
        gu = jnp.dot(x.astype(BF16), wgu_bf[...], preferred_element_type=F32) + bgu_ref[0, 0]
        gate = jnp.minimum(gu[:, :d_ff], SWIGLU_LIMIT)
        up = jnp.clip(gu[:, d_ff:], -SWIGLU_LIMIT, SWIGLU_LIMIT)
        glu = gate * jax.nn.sigmoid(gate * SWIGLU_ALPHA)
        act = ((up + 1.0) * glu).astype(BF16)
        y = jnp.dot(act, wd_bf[...], preferred_element_type=F32) + bd_ref[0, 0]
        for ch in range(nch):
            y_ref[pl.ds(ch, bm, stride=nch), :] = y[:, ch * LANES:(ch + 1) * LANES]

    def drain(s):
        idx_copy(last_blk, 1 - s).wait()
        rows_done(s).wait()

    for s in range(2):
        pl.when((lax.rem(i, 2) == s) & (i <= nv))(functools.partial(step, s))
    for s in range(2):
        pl.when((lax.rem(i, 2) == s) & (i == nv))(functools.partial(drain, s))

    @pl.when(i > nv)
    def _():
        y_ref[...] = jnp.zeros_like(y_ref)


def _experts(h8, sorted_tok, block_expert, block_base, n_valid, layer, w_gu, b_gu, w_down, b_down, n_blocks):
    bm = MOE_BM
    depth, n_e, d_ff, d = w_down.shape

    def cidx(i):
        return jnp.maximum(i - 1, 0)

    def wmap(i, be, base, nv):
        return (layer, be[cidx(i)], 0, 0)

    grid_spec = pltpu.PrefetchScalarGridSpec(
        num_scalar_prefetch=3,
        grid=(n_blocks + 1,),
        in_specs=[
            pl.BlockSpec(memory_space=pl.ANY),
            pl.BlockSpec(memory_space=pl.ANY),
            pl.BlockSpec((1, 1, d, 2 * d_ff), wmap),
            pl.BlockSpec((1, 1, 1, 2 * d_ff), wmap),
            pl.BlockSpec((1, 1, d_ff, d), wmap),
            pl.BlockSpec((1, 1, 1, d), wmap),
        ],
        out_specs=pl.BlockSpec((bm * ROW_CHUNKS, LANES), lambda i, be, base, nv: (cidx(i), 0)),
        scratch_shapes=[
            pltpu.SMEM((bm + LANES,), I32),
            pltpu.SMEM((bm + LANES,), I32),
            pltpu.VMEM((bm * ROW_CHUNKS, LANES), F32),
            pltpu.VMEM((bm * ROW_CHUNKS, LANES), F32),
            pltpu.VMEM((d, 2 * d_ff), BF16),
            pltpu.VMEM((d_ff, d), BF16),
            pltpu.SemaphoreType.DMA((2,)),
            pltpu.SemaphoreType.DMA((2,)),
        ],
    )
    return pl.pallas_call(
        functools.partial(_expert_kernel, n_blocks=n_blocks),
        grid_spec=grid_spec,
        out_shape=jax.ShapeDtypeStruct((n_blocks * bm * ROW_CHUNKS, LANES), F32),
        compiler_params=_params(1),
        name="moe_experts",
    )(block_expert, block_base, n_valid, sorted_tok, h8, w_gu, b_gu.reshape(depth, n_e, 1, -1), w_down,
      b_down.reshape(depth, n_e, 1, -1))


def _combine_kernel(didx_hbm, ys_hbm, x_ref, gt_ref, fg_ref, *refs, n_tiles, final_norm, fuse_qkv):
    if fuse_qkv:
        qkv_in, refs = refs[:5], refs[5:]
        o_ref, q_ref, k_ref, v_ref = refs[:4]
        idx0, idx1, buf0, buf1, isem, gsem = refs[4:]
    else:
        o_ref, idx0, idx1, buf0, buf1, isem, gsem = refs
    i = pl.program_id(0)
    last = n_tiles - 1
    idx_smem = (idx0, idx1)
    buf = (buf0, buf1)
    rows = idx0.shape[0]
    tc = rows // TOP_K
    nch = ROW_CHUNKS

    def idx_copy(blk, s):
        return pltpu.make_async_copy(didx_hbm.at[pl.ds(blk * rows, rows)], idx_smem[s], isem.at[s])

    def rows_done(s):
        return pltpu.make_async_copy(ys_hbm.at[pl.ds(0, rows * nch), :], buf[s], gsem.at[s])

    @pl.when(i == 0)
    def _():
        idx_copy(0, 0).start()
        rows_done(1).start()

    def step(s):
        idx_copy(jnp.minimum(i, last), s).wait()
        rows_done(1 - s).wait()
        idx_copy(jnp.minimum(i + 1, last), 1 - s).start()
        for r in range(rows):
            src = ys_hbm.at[pl.ds(pl.multiple_of(idx_smem[s][r] * nch, nch), nch), :]
            pltpu.make_async_copy(src, buf[s].at[pl.ds(r * nch, nch), :], gsem.at[s]).start(priority=r % 2)
        acc = x_ref[...]
        gt = gt_ref[...]
        for kk in range(TOP_K):
            yk = jnp.concatenate([buf[1 - s][pl.ds(kk * tc * nch + ch, tc, stride=nch), :]
                                  for ch in range(nch)], axis=1)
            acc = acc + gt[:, kk:kk + 1] * yk
        if final_norm:
            acc = _rms(acc, fg_ref[...])
        o_ref[...] = acc
        if fuse_qkv:
            _qkv_store(acc, *qkv_in, q_ref, k_ref, v_ref)

    for s in range(2):
        pl.when(lax.rem(i, 2) == s)(functools.partial(step, s))

    @pl.when(i == n_tiles)
    def _():
        s_last = n_tiles % 2
        idx_copy(last, 1 - s_last).wait()
        rows_done(s_last).wait()


def _combine(ys8, dest_tiles, gates_t, x2d, final_g, qkv=None):
    t, d = x2d.shape
    tc = COMBINE_TILE
    n = t // tc
    rows = TOP_K * tc

    def cidx(i):
        return jnp.maximum(i - 1, 0)

    row_spec = pl.BlockSpec((tc, d), lambda i: (cidx(i), 0))
    fg = jnp.ones((1, d), F32) if final_g is None else final_g.reshape(1, -1).astype(F32)
    in_specs = [
        pl.BlockSpec(memory_space=pl.ANY),
        pl.BlockSpec(memory_space=pl.ANY),
        row_spec,
        pl.BlockSpec((tc, TOP_K), lambda i: (cidx(i), 0)),
        pl.BlockSpec((1, d), lambda i: (0, 0)),
    ]
    args = [dest_tiles, ys8, x2d, gates_t, fg]
    out_specs = [row_spec]
    out_shape = [jax.ShapeDtypeStruct((t, d), F32)]
    if qkv is not None:
        g_mix, w_qkv, b_qkv, seq = qkv
        cos_l, sin_l = _rope_tables(seq)
        spt = seq // tc
        qd, kvd = N_HEADS * HEAD_DIM, N_KV_HEADS * LANES
        in_specs += [
            pl.BlockSpec((1, d), lambda i: (0, 0)),
            pl.BlockSpec(w_qkv.shape, lambda i: (0, 0)),
            pl.BlockSpec((1, w_qkv.shape[1]), lambda i: (0, 0)),
            pl.BlockSpec((tc, LANES), lambda i: (cidx(i) % spt, 0)),
            pl.BlockSpec((tc, LANES), lambda i: (cidx(i) % spt, 0)),
        ]
        args += [g_mix.reshape(1, -1), w_qkv.astype(BF16), b_qkv.reshape(1, -1), cos_l, sin_l]
        out_specs += [pl.BlockSpec((tc, w), lambda i: (cidx(i), 0)) for w in (qd, kvd, kvd)]
        out_shape += [jax.ShapeDtypeStruct((t, w), BF16) for w in (qd, kvd, kvd)]
    return pl.pallas_call(
        functools.partial(_combine_kernel, n_tiles=n, final_norm=final_g is not None, fuse_qkv=qkv is not None),
        grid=(n + 1,),
        in_specs=in_specs,
        out_specs=out_specs,
        out_shape=out_shape,
        scratch_shapes=[
            pltpu.SMEM((rows,), I32),
            pltpu.SMEM((rows,), I32),
            pltpu.VMEM((rows * ROW_CHUNKS, LANES), F32),
            pltpu.VMEM((rows * ROW_CHUNKS, LANES), F32),
            pltpu.SemaphoreType.DMA((2,)),
            pltpu.SemaphoreType.DMA((2,)),
        ],
        compiler_params=_params(1),
        name="moe_combine",
    )(*args)


def _moe_layer(x2d, g, router_w, router_b, layer, w_gu, b_gu, w_down, b_down, final_g, next_qkv=None):
    t, d = x2d.shape
    bm = MOE_BM
    a = t * TOP_K
    n_blocks = a // bm + N_EXPERTS

    h, e_kt, gate_kt, rank_kt, cnt = _router(x2d, g, router_w, router_b)

    counts = cnt[:, 0].astype(I32)
    blocks_e = (counts + bm - 1) // bm
    blk_end = jnp.cumsum(blocks_e)
    blk_start = blk_end - blocks_e
    n_valid = blk_end[-1]
    tok_start = jnp.cumsum(counts) - counts

    a_id = jnp.arange(t, dtype=I32)[None, :] * TOP_K + jnp.arange(TOP_K, dtype=I32)[:, None]
    sorted_key = jnp.sort((e_kt * a + a_id).reshape(-1))
    sorted_tok = jnp.pad((sorted_key % a) // TOP_K, (0, bm + LANES))

    eids = jnp.arange(N_EXPERTS, dtype=I32)
    bid = jnp.arange(n_blocks, dtype=I32)
    be = jnp.minimum(jnp.sum((bid[:, None] >= blk_end[None, :]).astype(I32), axis=1), N_EXPERTS - 1)
    hot = be[:, None] == eids[None, :]
    base = (jnp.sum(jnp.where(hot, tok_start[None, :], 0), axis=1)
            + (bid - jnp.sum(jnp.where(hot, blk_start[None, :], 0), axis=1)) * bm)
    valid = bid < n_valid
    last_e = jnp.sum(jnp.where(bid == n_valid - 1, be, 0))
    block_expert = jnp.where(valid, be, last_e).astype(I32)
    block_base = jnp.where(valid, base, 0).astype(I32)

    ys = _experts(h, sorted_tok, block_expert, block_base, n_valid.reshape(1).astype(I32), layer,
                  w_gu, b_gu, w_down, b_down, n_blocks)

    dest = rank_kt
    for e in range(N_EXPERTS):
        dest = dest + jnp.where(e_kt == e, blk_start[e] * bm, 0)
    tc = COMBINE_TILE
    dest_tiles = dest.reshape(TOP_K, t // tc, tc).transpose(1, 0, 2).reshape(-1).astype(I32)
    return _combine(ys, dest_tiles, gate_kt.T, x2d, final_g, next_qkv)


def kernel(x, norm_mix_g, norm_ffn_g, fnet_w_o, fnet_b_o, attn_w_qkv, attn_b_qkv, attn_sink, attn_w_o, attn_b_o, router_w, router_b, expert_w_gu, expert_b_gu, expert_w_down, expert_b_down, final_norm_g):
    batch, seq, d = x.shape
    depth = norm_mix_g.shape[0]
    x2d = x.reshape(batch * seq, d)
    qkv = None
    for i in range(depth):
        j = i // 2
        if i % 2 == 0:
            x2d = _fourier_layer(x2d, norm_mix_g[i], fnet_w_o[j], fnet_b_o[j], batch, seq)
        else:
            x2d = _attention_layer(x2d, norm_mix_g[i], attn_w_qkv[j], attn_b_qkv[j], attn_sink[j],
                                   attn_w_o[j], attn_b_o[j], batch, seq, qkv)
        fg = final_norm_g if i == depth - 1 else None
        nxt = i + 1
        next_qkv = ((norm_mix_g[nxt], attn_w_qkv[nxt // 2], attn_b_qkv[nxt // 2], seq)
                    if nxt < depth and nxt % 2 == 1 else None)
        x2d, *qkv = _moe_layer(x2d, norm_ffn_g[i], router_w[i], router_b[i], i, expert_w_gu, expert_b_gu,
                               expert_w_down, expert_b_down, fg, next_qkv)
        qkv = qkv or None
    return x2d.reshape(batch, seq, d)
```
